```python
import math
import jax, jax.numpy as jnp
from jax import lax
import numpy as np

D_MODEL = 1024
BATCH = 4
SEQ = 4096
DEPTH = 2
DEC_BATCH = 8
DEC_SEQ = 8192
PAST_LEN = 128

D_MIX = D_MODEL
D_CONV = D_MIX // 2
D_RNN = D_MIX - D_CONV
D_IN = 2 * D_CONV + 2 * D_RNN
CONV_WIDTH = 31
RNN_HEADS = 8
RNN_HEAD_DIM = D_RNN // RNN_HEADS
RNN_CONV_WIDTH = 4
LRU_C = 8.0
N_DIR = 2
D_FF = 2816
FFN_CONV_WIDTH = 3
LN_EPS = 1e-5
DEEPNORM_ALPHA = (2.0 * DEPTH) ** 0.25
DEEPNORM_BETA = (8.0 * DEPTH) ** -0.25

kernel_name = "hymba_conformer_hawk_encoder"


def layer_norm(x, g, b):
    xf = x.astype(jnp.float32)
    mu = jnp.mean(xf, axis=-1, keepdims=True)
    var = jnp.mean(jnp.square(xf - mu), axis=-1, keepdims=True)
    y = (xf - mu) * lax.rsqrt(var + LN_EPS)
    return (y * g.astype(jnp.float32) + b.astype(jnp.float32)).astype(x.dtype)


def depthwise_conv(x, w, b, pad):
    y = lax.conv_general_dilated(
        x, w[:, None, :].astype(x.dtype), window_strides=(1,), padding=(pad,),
        dimension_numbers=("NWC", "WIO", "NWC"), feature_group_count=x.shape[-1])
    return y + b.astype(x.dtype)


def _lin_rec_combine(e1, e2):
    a1, b1 = e1
    a2, b2 = e2
    return (a1 * a2, a2 * b1 + b2)


def rg_lru(x, w_a, b_a, w_x, b_x, lam, reverse):
    bsz, seq, _ = x.shape
    xh = x.reshape(bsz, seq, RNN_HEADS, RNN_HEAD_DIM)
    gate_a = jnp.einsum("bshi,hij->bshj", xh, w_a).reshape(bsz, seq, D_RNN) + b_a
    gate_x = jnp.einsum("bshi,hij->bshj", xh, w_x).reshape(bsz, seq, D_RNN) + b_x
    r = jax.nn.sigmoid(gate_a.astype(jnp.float32))
    i = jax.nn.sigmoid(gate_x.astype(jnp.float32))
    log_a = -LRU_C * r * jax.nn.softplus(-lam.astype(jnp.float32))
    a = jnp.exp(log_a)
    b = jnp.sqrt(-jnp.expm1(2.0 * log_a)) * (i * x.astype(jnp.float32))
    _, h = lax.associative_scan(_lin_rec_combine, (a, b), axis=1, reverse=reverse)
    return h.astype(x.dtype)


def encoder_layer(x, w_in, conv_dw_w, conv_dw_b, conv_ln_g, conv_ln_b,
                  rnn_conv_w, rnn_conv_b, rg_w_a, rg_b_a, rg_w_x, rg_b_x, rg_lambda,
                  w_out, ln1_g, ln1_b, w_up, ffn_dw_w, ffn_dw_b, w_down, ln2_g, ln2_b):
    h = x @ w_in.astype(x.dtype)
    c_val, c_gate, r_x, r_gate = jnp.split(
        h, [D_CONV, 2 * D_CONV, 2 * D_CONV + D_RNN], axis=-1)
    c = c_val * jax.nn.sigmoid(c_gate)
    c = depthwise_conv(c, conv_dw_w, conv_dw_b, (CONV_WIDTH // 2, CONV_WIDTH // 2))
    c = jax.nn.silu(layer_norm(c, conv_ln_g, conv_ln_b))
    x_f = depthwise_conv(r_x, rnn_conv_w[0], rnn_conv_b[0], (RNN_CONV_WIDTH - 1, 0))
    x_b = depthwise_conv(r_x, rnn_conv_w[1], rnn_conv_b[1], (0, RNN_CONV_WIDTH - 1))
    rec = (rg_lru(x_f, rg_w_a[0], rg_b_a[0], rg_w_x[0], rg_b_x[0], rg_lambda[0], False)
           + rg_lru(x_b, rg_w_a[1], rg_b_a[1], rg_w_x[1], rg_b_x[1], rg_lambda[1], True))
    rec = rec * jax.nn.gelu(r_gate)
    mix = jnp.concatenate([c, rec], axis=-1) @ w_out.astype(x.dtype)
    x = layer_norm(DEEPNORM_ALPHA * x + mix, ln1_g, ln1_b)
    u = x @ w_up.astype(x.dtype)
    u = depthwise_conv(u, ffn_dw_w, ffn_dw_b, (FFN_CONV_WIDTH // 2, FFN_CONV_WIDTH // 2))
    val, gate = jnp.split(u, [D_FF], axis=-1)
    f = (jax.nn.gelu(gate) * val) @ w_down.astype(x.dtype)
    return layer_norm(DEEPNORM_ALPHA * x + f, ln2_g, ln2_b)


def encoder_trunk(x, ln_in_g, ln_in_b, w_in, conv_dw_w, conv_dw_b, conv_ln_g, conv_ln_b,
                  rnn_conv_w, rnn_conv_b, rg_w_a, rg_b_a, rg_w_x, rg_b_x, rg_lambda,
                  w_out, ln1_g, ln1_b, w_up, ffn_dw_w, ffn_dw_b, w_down, ln2_g, ln2_b):
    x = layer_norm(x, ln_in_g, ln_in_b)
    for l in range(DEPTH):
        x = encoder_layer(x, w_in[l], conv_dw_w[l], conv_dw_b[l], conv_ln_g[l], conv_ln_b[l],
                          rnn_conv_w[l], rnn_conv_b[l], rg_w_a[l], rg_b_a[l], rg_w_x[l],
                          rg_b_x[l], rg_lambda[l], w_out[l], ln1_g[l], ln1_b[l], w_up[l],
                          ffn_dw_w[l], ffn_dw_b[l], w_down[l], ln2_g[l], ln2_b[l])
    return x


def setup_inputs(seed: int = 0) -> dict:
    key = jax.random.key(seed)
    ks = jax.random.split(key, 26)
    f32 = jnp.float32

    def nrm(k, shape, scale):
        return jax.random.normal(k, shape, f32) * scale

    a0 = jax.random.uniform(ks[14], (DEPTH, N_DIR, D_RNN), f32, minval=0.9, maxval=0.999)
    s = a0 ** (1.0 / LRU_C)
    rg_lambda = jnp.log(s) - jnp.log1p(-s)
    return {
        "x_prompt": nrm(ks[0], (BATCH, SEQ, D_MODEL), 1.0),
        "x_sample": nrm(ks[1], (DEC_BATCH, DEC_SEQ, D_MODEL), 1.0),
        "ln_in_g": 1.0 + nrm(ks[2], (D_MODEL,), 0.02),
        "ln_in_b": nrm(ks[3], (D_MODEL,), 0.02),
        "w_in": nrm(ks[4], (DEPTH, D_MODEL, D_IN), D_MODEL ** -0.5),
        "conv_dw_w": nrm(ks[5], (DEPTH, CONV_WIDTH, D_CONV), CONV_WIDTH ** -0.5),
        "conv_dw_b": nrm(ks[6], (DEPTH, D_CONV), 0.02),
        "conv_ln_g": 1.0 + nrm(ks[7], (DEPTH, D_CONV), 0.02),
        "conv_ln_b": nrm(ks[8], (DEPTH, D_CONV), 0.02),
        "rnn_conv_w": nrm(ks[9], (DEPTH, N_DIR, RNN_CONV_WIDTH, D_RNN), RNN_CONV_WIDTH ** -0.5),
        "rnn_conv_b": nrm(ks[10], (DEPTH, N_DIR, D_RNN), 0.02),
        "rg_w_a": nrm(ks[11], (DEPTH, N_DIR, RNN_HEADS, RNN_HEAD_DIM, RNN_HEAD_DIM), RNN_HEAD_DIM ** -0.5),
        "rg_b_a": nrm(ks[12], (DEPTH, N_DIR, D_RNN), 0.02),
        "rg_w_x": nrm(ks[13], (DEPTH, N_DIR, RNN_HEADS, RNN_HEAD_DIM, RNN_HEAD_DIM), RNN_HEAD_DIM ** -0.5),
        "rg_b_x": nrm(ks[15], (DEPTH, N_DIR, D_RNN), 0.02),
        "rg_lambda": rg_lambda,
        "w_out": nrm(ks[16], (DEPTH, D_MIX, D_MODEL), DEEPNORM_BETA * D_MIX ** -0.5),
        "ln1_g": 1.0 + nrm(ks[17], (DEPTH, D_MODEL), 0.02),
        "ln1_b": nrm(ks[18], (DEPTH, D_MODEL), 0.02),
        "w_up": nrm(ks[19], (DEPTH, D_MODEL, 2 * D_FF), D_MODEL ** -0.5),
        "ffn_dw_w": nrm(ks[20], (DEPTH, FFN_CONV_WIDTH, 2 * D_FF), FFN_CONV_WIDTH ** -0.5),
        "ffn_dw_b": nrm(ks[21], (DEPTH, 2 * D_FF), 0.02),
        "w_down": nrm(ks[22], (DEPTH, D_FF, D_MODEL), DEEPNORM_BETA * D_FF ** -0.5),
        "ln2_g": 1.0 + nrm(ks[23], (DEPTH, D_MODEL), 0.02),
        "ln2_b": nrm(ks[24], (DEPTH, D_MODEL), 0.02),
    }


def reference(x_prompt, x_sample, ln_in_g, ln_in_b, w_in, conv_dw_w, conv_dw_b, conv_ln_g,
              conv_ln_b, rnn_conv_w, rnn_conv_b, rg_w_a, rg_b_a, rg_w_x, rg_b_x, rg_lambda,
              w_out, ln1_g, ln1_b, w_up, ffn_dw_w, ffn_dw_b, w_down, ln2_g, ln2_b):
    y_prompt = encoder_trunk(x_prompt, ln_in_g, ln_in_b, w_in, conv_dw_w, conv_dw_b, conv_ln_g,
                             conv_ln_b, rnn_conv_w, rnn_conv_b, rg_w_a, rg_b_a, rg_w_x, rg_b_x,
                             rg_lambda, w_out, ln1_g, ln1_b, w_up, ffn_dw_w, ffn_dw_b, w_down,
                             ln2_g, ln2_b)
    y_sample = encoder_trunk(x_sample, ln_in_g, ln_in_b, w_in, conv_dw_w, conv_dw_b, conv_ln_g,
                             conv_ln_b, rnn_conv_w, rnn_conv_b, rg_w_a, rg_b_a, rg_w_x, rg_b_x,
                             rg_lambda, w_out, ln1_g, ln1_b, w_up, ffn_dw_w, ffn_dw_b, w_down,
                             ln2_g, ln2_b)
    return (y_prompt, y_sample)
```

```python
import functools
import math

import jax
import jax.numpy as jnp
from jax import lax
from jax.experimental import pallas as pl
from jax.experimental.pallas import tpu as pltpu

D_MODEL = 1024
D_CONV = 512
D_RNN = 512
D_IN = 2 * D_CONV + 2 * D_RNN
CONV_WIDTH = 31
RNN_HEADS = 8
RNN_HEAD_DIM = D_RNN // RNN_HEADS
RNN_CONV_WIDTH = 4
LRU_C = 8.0
D_FF = 2816
FFN_CHUNK = 256
N_FFN_CHUNKS = D_FF // FFN_CHUNK
LN_EPS = 1e-5
DEPTH = 2
DEEPNORM_ALPHA = (2.0 * DEPTH) ** 0.25

NV = 8
CONV_HALO_ROWS = 128
FFN_HALO_ROWS = NV
TILE_ROWS = 512
VMEM_LIMIT = 56 * 1024 * 1024

F32 = jnp.float32
BF16 = jnp.bfloat16


def _ln(x, g, b):
    mu = jnp.mean(x, axis=-1, keepdims=True)
    xc = x - mu
    var = jnp.mean(xc * xc, axis=-1, keepdims=True)
    return xc * lax.rsqrt(var + LN_EPS) * g + b


def _const_spec(shape):
    nd = len(shape)
    return pl.BlockSpec(shape, lambda *_: (0,) * nd, pipeline_mode=pl.Buffered(1))


def _params(n_axes=1):
    return pltpu.CompilerParams(
        dimension_semantics=("arbitrary",) * n_axes, vmem_limit_bytes=VMEM_LIMIT)


def _inproj_body(x_ref, lg_ref, lb_ref, w_ref, *out_refs, first):
    x = x_ref[...]
    if first:
        xn_ref, g_ref, rx_ref, gr_ref = out_refs
        x = _ln(x, lg_ref[...], lb_ref[...])
        xn_ref[...] = x
    else:
        g_ref, rx_ref, gr_ref = out_refs
    h = jnp.dot(x.astype(BF16), w_ref[...], preferred_element_type=F32)
    g_ref[...] = h[:, :D_CONV] * jax.nn.sigmoid(h[:, D_CONV:2 * D_CONV])
    rx_ref[...] = h[:, 2 * D_CONV:2 * D_CONV + D_RNN]
    gr_ref[...] = jax.nn.gelu(h[:, 2 * D_CONV + D_RNN:])


def _inproj_first(x_nat, ln_g, ln_b, w_in):
    _, sv, _ = x_nat.shape
    tt = TILE_ROWS
    grid = (NV, sv // tt)
    widths = (D_MODEL, D_CONV, D_RNN, D_RNN)
    outs = pl.pallas_call(
        functools.partial(_inproj_body, first=True),
        grid=grid,
        in_specs=[
            pl.BlockSpec((None, tt, D_MODEL), lambda v, i: (v, i, 0)),
            _const_spec((1, D_MODEL)), _const_spec((1, D_MODEL)),
            _const_spec((D_MODEL, D_IN)),
        ],
        out_specs=[pl.BlockSpec((tt, w), lambda v, i: (i, v)) for w in widths],
        out_shape=[jax.ShapeDtypeStruct((sv, NV * w), F32) for w in widths],
        compiler_params=_params(2),
        name="inproj_first",
    )(x_nat, ln_g, ln_b, w_in)
    return [o.reshape(sv * NV, w) for o, w in zip(outs, widths)]


def _inproj(x, ln_g, ln_b, w_in):
    rows = x.shape[0]
    r = TILE_ROWS
    widths = (D_CONV, D_RNN, D_RNN)
    return pl.pallas_call(
        functools.partial(_inproj_body, first=False),
        grid=(rows // r,),
        in_specs=[
            pl.BlockSpec((r, D_MODEL), lambda i: (i, 0)),
            _const_spec((1, D_MODEL)), _const_spec((1, D_MODEL)),
            _const_spec((D_MODEL, D_IN)),
        ],
        out_specs=[pl.BlockSpec((r, w), lambda i: (i, 0)) for w in widths],
        out_shape=[jax.ShapeDtypeStruct((rows, w), F32) for w in widths],
        compiler_params=_params(1),
        name="inproj",
    )(x, ln_g, ln_b, w_in)


def _fill_with_halo(buf, prev_ref, main_ref, next_ref, halo, rows, chunks):
    i = pl.program_id(0)
    last = pl.num_programs(0) - 1
    width = buf.shape[1]
    buf[halo:halo + rows, :] = main_ref[...]
    piece = lax.broadcasted_iota(jnp.int32, (halo, width), 0) & (chunks - 1)

    @pl.when(i > 0)
    def _():
        buf[0:halo, :] = prev_ref[...]

    @pl.when(i == 0)
    def _():
        if chunks == 1:
            buf[0:halo, :] = jnp.zeros((halo, width), F32)
        else:
            rolled = pltpu.roll(prev_ref[...], 1, axis=0)
            buf[0:halo, :] = jnp.where(piece != 0, rolled, 0.0)

    @pl.when(i < last)
    def _():
        buf[halo + rows:, :] = next_ref[...]

    @pl.when(i == last)
    def _():
        if chunks == 1:
            buf[halo + rows:, :] = jnp.zeros((halo, width), F32)
        else:
            rolled = pltpu.roll(next_ref[...], halo - 1, axis=0)
            buf[halo + rows:, :] = jnp.where(piece != chunks - 1, rolled, 0.0)


def _halo_specs(rows, halo, width, n_rows_total):
    per = rows // halo
    nblk = n_rows_total // halo
    return [
        pl.BlockSpec((halo, width), lambda i: ((i * per + nblk - 1) % nblk, 0)),
        pl.BlockSpec((rows, width), lambda i: (i, 0)),
        pl.BlockSpec((halo, width), lambda i: (((i + 1) * per) % nblk, 0)),
    ]


CONV_ROW_CHUNK = 32


def _mixer_body(gp_ref, g_ref, gn_ref, rp_ref, r_ref, rn_ref,
                cw_ref, cb_ref, clg_ref, clb_ref, rw_ref, rb_ref,
                wg_ref, ba_ref, bx_ref, lam_ref,
                c_ref, hf_ref, pf_ref, hb_ref, pb_ref,
                gbuf, rbuf, a_buf, b_buf, *, rows, chunks):
    halo = CONV_HALO_ROWS
    _fill_with_halo(gbuf, gp_ref, g_ref, gn_ref, halo, rows, chunks)
    _fill_with_halo(rbuf, rp_ref, r_ref, rn_ref, halo, rows, chunks)

    half = CONV_WIDTH // 2

    def conv_chunk(j, _):
        base = pl.multiple_of(j * CONV_ROW_CHUNK, CONV_ROW_CHUNK)
        acc = jnp.broadcast_to(cb_ref[...], (CONV_ROW_CHUNK, D_CONV))
        for k in range(CONV_WIDTH):
            start = base + (halo + (k - half) * NV)
            acc = acc + gbuf[pl.ds(start, CONV_ROW_CHUNK), :] * cw_ref[k:k + 1, :]
        y = _ln(acc, clg_ref[...], clb_ref[...])
        c_ref[pl.ds(base, CONV_ROW_CHUNK), :] = jax.nn.silu(y)
        return 0

    lax.fori_loop(0, rows // CONV_ROW_CHUNK, conv_chunk, 0)

    nsteps = rows // NV
    for d in range(2):
        x = jnp.broadcast_to(rb_ref[d:d + 1, :], (rows, D_RNN))
        for k in range(RNN_CONV_WIDTH):
            shift = (k - (RNN_CONV_WIDTH - 1)) if d == 0 else k
            start = halo + shift * NV
            x = x + rbuf[start:start + rows, :] * rw_ref[d * RNN_CONV_WIDTH + k:
                                                         d * RNN_CONV_WIDTH + k + 1, :]
        xb = x.astype(BF16)
        half_w = D_RNN // 2
        parts = [jnp.dot(xb[:, hf * half_w:(hf + 1) * half_w], wg_ref[d, hf],
                         preferred_element_type=F32) for hf in range(2)]
        gate_a = jnp.concatenate([p[:, :half_w] for p in parts], axis=-1) + ba_ref[d:d + 1, :]
        gate_x = jnp.concatenate([p[:, half_w:] for p in parts], axis=-1) + bx_ref[d:d + 1, :]
        z = -lam_ref[d:d + 1, :]
        softplus = jnp.maximum(z, 0.0) + jnp.log1p(jnp.exp(-jnp.abs(z)))
        log_a = (-LRU_C) * jax.nn.sigmoid(gate_a) * softplus
        a = jnp.exp(log_a)
        one_minus_a2 = -jnp.tanh(log_a) * (a * a + 1.0)
        b = jnp.sqrt(one_minus_a2) * (jax.nn.sigmoid(gate_x) * x)
        a_buf[d] = a
        b_buf[d] = b

    def scan_step(t, carry):
        hf, pf, hb, pb = carry
        rf = pl.multiple_of(t * NV, NV)
        rb = pl.multiple_of((nsteps - 1 - t) * NV, NV)
        af = a_buf[0, pl.ds(rf, NV), :]
        hf = af * hf + b_buf[0, pl.ds(rf, NV), :]
        pf = af * pf
        hf_ref[pl.ds(rf, NV), :] = hf
        pf_ref[pl.ds(rf, NV), :] = pf
        ab = a_buf[1, pl.ds(rb, NV), :]
        hb = ab * hb + b_buf[1, pl.ds(rb, NV), :]
        pb = ab * pb
        hb_ref[pl.ds(rb, NV), :] = hb
        pb_ref[pl.ds(rb, NV), :] = pb
        return hf, pf, hb, pb

    zeros = jnp.zeros((NV, D_RNN), F32)
    ones = jnp.ones((NV, D_RNN), F32)
    lax.fori_loop(0, nsteps, scan_step, (zeros, ones, zeros, ones), unroll=4)


def _mixer(g, rx, p, chunks):
    rows_total = g.shape[0]
    rows = TILE_ROWS
    halo = CONV_HALO_ROWS
    small = [p["conv_w"], p["conv_b"], p["conv_ln_g"], p["conv_ln_b"],
             p["rnn_conv_w"], p["rnn_conv_b"], p["w_gate"], p["b_a"], p["b_x"], p["lam"]]
    out_spec = pl.BlockSpec((rows, D_RNN), lambda i: (i, 0))
    return pl.pallas_call(
        functools.partial(_mixer_body, rows=rows, chunks=chunks),
        grid=(rows_total // rows,),
        in_specs=(_halo_specs(rows, halo, D_CONV, rows_total)
                  + _halo_specs(rows, halo, D_RNN, rows_total)
                  + [_const_spec(a.shape) for a in small]),
        out_specs=[out_spec] * 5,
        out_shape=[jax.ShapeDtypeStruct((rows_total, D_RNN), F32)] * 5,
        scratch_shapes=[
            pltpu.VMEM((rows + 2 * halo, D_CONV), F32),
            pltpu.VMEM((rows + 2 * halo, D_RNN), F32),
            pltpu.VMEM((2, rows, D_RNN), F32),
            pltpu.VMEM((2, rows, D_RNN), F32),
        ],
        compiler_params=_params(1),
        name="mixer",
    )(g, g, g, rx, rx, rx, *small)


def _carry_body(hfe_ref, pfe_ref, hbe_ref, pbe_ref,
                cf_ref, qf_ref, cb_ref, qb_ref, ef_ref, eb_ref,
                sf, uf, sb, ub):
    @pl.when(pl.program_id(0) == 0)
    def _():
        sf[...] = jnp.zeros_like(sf)
        sb[...] = jnp.zeros_like(sb)
        uf[...] = jnp.ones_like(uf)
        ub[...] = jnp.ones_like(ub)

    cf_ref[...] = sf[...]
    qf_ref[...] = uf[...]
    cb_ref[...] = sb[...]
    qb_ref[...] = ub[...]
    sf[...] = hfe_ref[...] + pfe_ref[...] * sf[...]
    uf[...] = pfe_ref[...] * uf[...]
    sb[...] = hbe_ref[...] + pbe_ref[...] * sb[...]
    ub[...] = pbe_ref[...] * ub[...]
    ef_ref[...] = sf[...]
    eb_ref[...] = sb[...]


def _carry(hf, pf, hb, pb):
    rows_total = hf.shape[0]
    nt = rows_total // TILE_ROWS
    per = TILE_ROWS // NV
    blk = (NV, D_RNN)
    fwd_end = pl.BlockSpec(blk, lambda i: ((i + 1) * per - 1, 0))
    bwd_end = pl.BlockSpec(blk, lambda i: ((nt - 1 - i) * per, 0))
    fwd_out = pl.BlockSpec(blk, lambda i: (i, 0))
    bwd_out = pl.BlockSpec(blk, lambda i: (nt - 1 - i, 0))
    return pl.pallas_call(
        _carry_body,
        grid=(nt,),
        in_specs=[fwd_end, fwd_end, bwd_end, bwd_end],
        out_specs=[fwd_out, fwd_out, bwd_out, bwd_out,
                   pl.BlockSpec(blk, lambda i: (0, 0)), pl.BlockSpec(blk, lambda i: (0, 0))],
        out_shape=[jax.ShapeDtypeStruct((nt * NV, D_RNN), F32)] * 4
        + [jax.ShapeDtypeStruct(blk, F32)] * 2,
        scratch_shapes=[pltpu.VMEM(blk, F32)] * 4,
        compiler_params=_params(1),
        name="carry",
    )(hf, pf, hb, pb)


def _outproj_body(x_ref, c_ref, hf_ref, pf_ref, hb_ref, pb_ref, gr_ref,
                  cf_ref, qf_ref, cb_ref, qb_ref, ef_ref, eb_ref,
                  wo_ref, lg_ref, lb_ref, o_ref, *, rows, chunks):
    cf = cf_ref[...]
    cb = cb_ref[...]
    if chunks > 1:
        piece = lax.broadcasted_iota(jnp.int32, (NV, D_RNN), 0) & (chunks - 1)
        ef = jnp.where(piece != 0, pltpu.roll(ef_ref[...], 1, axis=0), 0.0)
        eb = jnp.where(piece != chunks - 1, pltpu.roll(eb_ref[...], NV - 1, axis=0), 0.0)
        cf = cf + qf_ref[...] * ef
        cb = cb + qb_ref[...] * eb
    nsteps = rows // NV
    shape3 = (nsteps, NV, D_RNN)
    rec = (hf_ref[...].reshape(shape3) + pf_ref[...].reshape(shape3) * cf[None]
           + hb_ref[...].reshape(shape3) + pb_ref[...].reshape(shape3) * cb[None])
    rec = rec.reshape(rows, D_RNN) * gr_ref[...]
    mix = (jnp.dot(c_ref[...].astype(BF16), wo_ref[0:D_CONV, :], preferred_element_type=F32)
           + jnp.dot(rec.astype(BF16), wo_ref[D_CONV:, :], preferred_element_type=F32))
    o_ref[...] = _ln(DEEPNORM_ALPHA * x_ref[...] + mix, lg_ref[...], lb_ref[...])


def _outproj(x, c, hf, pf, hb, pb, gr, carries, p, chunks):
    rows_total = x.shape[0]
    rows = TILE_ROWS
    cf, qf, cb, qb, ef, eb = carries
    blk = (NV, D_RNN)
    wide = pl.BlockSpec((rows, D_MODEL), lambda i: (i, 0))
    narrow = pl.BlockSpec((rows, D_RNN), lambda i: (i, 0))
    tile_state = pl.BlockSpec(blk, lambda i: (i, 0))
    return pl.pallas_call(
        functools.partial(_outproj_body, rows=rows, chunks=chunks),
        grid=(rows_total // rows,),
        in_specs=[wide] + [narrow] * 6 + [tile_state] * 4 + [_const_spec(blk)] * 2
        + [_const_spec((D_MODEL, D_MODEL)), _const_spec((1, D_MODEL)), _const_spec((1, D_MODEL))],
        out_specs=wide,
        out_shape=jax.ShapeDtypeStruct((rows_total, D_MODEL), F32),
        compiler_params=_params(1),
        name="outproj",
    )(x, c, hf, pf, hb, pb, gr, cf, qf, cb, qb, ef, eb, p["w_out"], p["ln1_g"], p["ln1_b"])


def _ffn_body(xp_ref, x_ref, xn_ref, wu_ref, dw_ref, db_ref, wd_ref, lg_ref, lb_ref,
              o_ref, xbuf, acc, *, rows, chunks):
    halo = FFN_HALO_ROWS
    _fill_with_halo(xbuf, xp_ref, x_ref, xn_ref, halo, rows, chunks)
    xe = xbuf[...].astype(BF16)
    acc[...] = jnp.zeros_like(acc)

    def conv3(u, idx):
        w = dw_ref[idx]
        return (u[0:rows] * w[0:1, :] + u[NV:NV + rows] * w[1:2, :]
                + u[2 * NV:2 * NV + rows] * w[2:3, :] + db_ref[idx])

    def chunk(j, _):
        uv = jnp.dot(xe, wu_ref[j], preferred_element_type=F32)
        ug = jnp.dot(xe, wu_ref[N_FFN_CHUNKS + j], preferred_element_type=F32)
        val = conv3(uv, j)
        gate = conv3(ug, N_FFN_CHUNKS + j)
        prod = (jax.nn.gelu(gate) * val).astype(BF16)
        acc[...] += jnp.dot(prod, wd_ref[j], preferred_element_type=F32)
        return 0

    lax.fori_loop(0, N_FFN_CHUNKS, chunk, 0)
    o_ref[...] = _ln(DEEPNORM_ALPHA * x_ref[...] + acc[...], lg_ref[...], lb_ref[...])


def _ffn(x, p, chunks):
    rows_total = x.shape[0]
    rows = TILE_ROWS
    small = [p["w_up"], p["ffn_dw_w"], p["ffn_dw_b"], p["w_down"], p["ln2_g"], p["ln2_b"]]
    return pl.pallas_call(
        functools.partial(_ffn_body, rows=rows, chunks=chunks),
        grid=(rows_total // rows,),
        in_specs=_halo_specs(rows, FFN_HALO_ROWS, D_MODEL, rows_total)
        + [_const_spec(a.shape) for a in small],
        out_specs=pl.BlockSpec((rows, D_MODEL), lambda i: (i, 0)),
        out_shape=jax.ShapeDtypeStruct((rows_total, D_MODEL), F32),
        scratch_shapes=[
            pltpu.VMEM((rows + 2 * FFN_HALO_ROWS, D_MODEL), F32),
            pltpu.VMEM((rows, D_MODEL), F32),
        ],
        compiler_params=_params(1),
        name="ffn",
    )(x, x, x, *small)


def _copy_body(x_ref, o_ref):
    o_ref[...] = x_ref[...]


def _to_natural(x, sv):
    tt = TILE_ROWS
    return pl.pallas_call(
        _copy_body,
        grid=(NV, sv // tt),
        in_specs=[pl.BlockSpec((tt, D_MODEL), lambda v, i: (i, v))],
        out_specs=pl.BlockSpec((None, tt, D_MODEL), lambda v, i: (v, i, 0)),
        out_shape=jax.ShapeDtypeStruct((NV, sv, D_MODEL), F32),
        compiler_params=_params(2),
        name="to_natural",
    )(x.reshape(sv, NV * D_MODEL))


def _block_diag_gate(w_a, w_x):
    heads_per_half = RNN_HEADS // 2
    half_w = D_RNN // 2

    def bd(w):
        eye = jnp.eye(heads_per_half, dtype=w.dtype)
        return jnp.einsum("hij,hg->higj", w, eye).reshape(half_w, half_w)

    out = []
    for d in range(2):
        halves = []
        for hf in range(2):
            sl = slice(hf * heads_per_half, (hf + 1) * heads_per_half)
            halves.append(jnp.concatenate([bd(w_a[d, sl]), bd(w_x[d, sl])], axis=-1))
        out.append(jnp.stack(halves))
    return jnp.stack(out).astype(BF16)


def _layer_params(l, w_in, conv_dw_w, conv_dw_b, conv_ln_g, conv_ln_b, rnn_conv_w, rnn_conv_b,
                  rg_w_a, rg_b_a, rg_w_x, rg_b_x, rg_lambda, w_out, ln1_g, ln1_b, w_up,
                  ffn_dw_w, ffn_dw_b, w_down, ln2_g, ln2_b):
    nc = N_FFN_CHUNKS
    row = lambda a: a.reshape(1, -1)
    return {
        "w_in": w_in[l].astype(BF16),
        "conv_w": conv_dw_w[l], "conv_b": row(conv_dw_b[l]),
        "conv_ln_g": row(conv_ln_g[l]), "conv_ln_b": row(conv_ln_b[l]),
        "rnn_conv_w": rnn_conv_w[l].reshape(2 * RNN_CONV_WIDTH, D_RNN),
        "rnn_conv_b": rnn_conv_b[l],
        "w_gate": _block_diag_gate(rg_w_a[l], rg_w_x[l]),
        "b_a": rg_b_a[l], "b_x": rg_b_x[l], "lam": rg_lambda[l],
        "w_out": w_out[l].astype(BF16), "ln1_g": row(ln1_g[l]), "ln1_b": row(ln1_b[l]),
        "w_up": w_up[l].astype(BF16).reshape(D_MODEL, 2 * nc, FFN_CHUNK).transpose(1, 0, 2),
        "ffn_dw_w": ffn_dw_w[l].reshape(3, 2 * nc, FFN_CHUNK).transpose(1, 0, 2),
        "ffn_dw_b": ffn_dw_b[l].reshape(2 * nc, 1, FFN_CHUNK),
        "w_down": w_down[l].astype(BF16).reshape(nc, FFN_CHUNK, D_MODEL),
        "ln2_g": row(ln2_g[l]), "ln2_b": row(ln2_b[l]),
    }


def _trunk(x, ln_in_g, ln_in_b, layers):
    bsz, seq, _ = x.shape
    chunks = NV // bsz
    sv = seq // chunks
    x_nat = x.reshape(NV, sv, D_MODEL)
    ln_g = ln_in_g.reshape(1, -1)
    ln_b = ln_in_b.reshape(1, -1)
    for l, p in enumerate(layers):
        if l == 0:
            xt, g, rx, gr = _inproj_first(x_nat, ln_g, ln_b, p["w_in"])
        else:
            g, rx, gr = _inproj(xt, ln_g, ln_b, p["w_in"])
        c, hf, pf, hb, pb = _mixer(g, rx, p, chunks)
        carries = _carry(hf, pf, hb, pb)
        x1 = _outproj(xt, c, hf, pf, hb, pb, gr, carries, p, chunks)
        xt = _ffn(x1, p, chunks)
    return _to_natural(xt, sv).reshape(bsz, seq, D_MODEL)


def kernel(x_prompt, x_sample, ln_in_g, ln_in_b, w_in, conv_dw_w, conv_dw_b, conv_ln_g,
           conv_ln_b, rnn_conv_w, rnn_conv_b, rg_w_a, rg_b_a, rg_w_x, rg_b_x, rg_lambda,
           w_out, ln1_g, ln1_b, w_up, ffn_dw_w, ffn_dw_b, w_down, ln2_g, ln2_b):
    layers = [
        _layer_params(l, w_in, conv_dw_w, conv_dw_b, conv_ln_g, conv_ln_b, rnn_conv_w,
                      rnn_conv_b, rg_w_a, rg_b_a, rg_w_x, rg_b_x, rg_lambda, w_out, ln1_g,
                      ln1_b, w_up, ffn_dw_w, ffn_dw_b, w_down, ln2_g, ln2_b)
        for l in range(DEPTH)
    ]
    y_prompt = _trunk(x_prompt, ln_in_g, ln_in_b, layers)
    y_sample = _trunk(x_sample, ln_in_g, ln_in_b, layers)
    return (y_prompt, y_sample)
```

```python
import functools
import math

import jax
import jax.numpy as jnp
from jax import lax
from jax.experimental import pallas as pl
from jax.experimental.pallas import tpu as pltpu

D_MODEL = 1024
D_CONV = 512
D_RNN = 512
D_IN = 2 * D_CONV + 2 * D_RNN
CONV_WIDTH = 31
RNN_HEADS = 8
RNN_HEAD_DIM = D_RNN // RNN_HEADS
RNN_CONV_WIDTH = 4
LRU_C = 8.0
D_FF = 2816
FFN_CHUNK = 256
N_FFN_CHUNKS = D_FF // FFN_CHUNK
LN_EPS = 1e-5
DEPTH = 2
DEEPNORM_ALPHA = (2.0 * DEPTH) ** 0.25

NV = 8
CONV_HALO_ROWS = 128
FFN_HALO_ROWS = NV
TILE_ROWS = 512
VMEM_LIMIT = 56 * 1024 * 1024

F32 = jnp.float32
BF16 = jnp.bfloat16


def _ln(x, g, b):
    mu = jnp.mean(x, axis=-1, keepdims=True)
    xc = x - mu
    var = jnp.mean(xc * xc, axis=-1, keepdims=True)
    return xc * lax.rsqrt(var + LN_EPS) * g + b


def _const_spec(shape):
    nd = len(shape)
    return pl.BlockSpec(shape, lambda *_: (0,) * nd, pipeline_mode=pl.Buffered(1))


def _params(n_axes=1):
    return pltpu.CompilerParams(
        dimension_semantics=("arbitrary",) * n_axes, vmem_limit_bytes=VMEM_LIMIT)


def _inproj_body(x_ref, lg_ref, lb_ref, w_ref, *out_refs, first):
    x = x_ref[...]
    if first:
        xn_ref, g_ref, rx_ref, gr_ref = out_refs
        x = _ln(x, lg_ref[...], lb_ref[...])
        xn_ref[...] = x
    else:
        g_ref, rx_ref, gr_ref = out_refs
    h = jnp.dot(x.astype(BF16), w_ref[...], preferred_element_type=F32)
    g_ref[...] = h[:, :D_CONV] * jax.nn.sigmoid(h[:, D_CONV:2 * D_CONV])
    rx_ref[...] = h[:, 2 * D_CONV:2 * D_CONV + D_RNN]
    gr_ref[...] = jax.nn.gelu(h[:, 2 * D_CONV + D_RNN:])


def _inproj(x, ln_g, ln_b, w_in, first):
    rows = x.shape[0]
    r = TILE_ROWS
    widths = ((D_MODEL,) if first else ()) + (D_CONV, D_RNN, D_RNN)
    return pl.pallas_call(
        functools.partial(_inproj_body, first=first),
        grid=(rows // r,),
        in_specs=[
            pl.BlockSpec((r, D_MODEL), lambda i: (i, 0)),
            _const_spec((1, D_MODEL)), _const_spec((1, D_MODEL)),
            _const_spec((D_MODEL, D_IN)),
        ],
        out_specs=[pl.BlockSpec((r, w), lambda i: (i, 0)) for w in widths],
        out_shape=[jax.ShapeDtypeStruct((rows, w), F32) for w in widths],
        compiler_params=_params(1),
        name="inproj",
    )(x, ln_g, ln_b, w_in)


def _fill_with_halo(buf, prev_ref, main_ref, next_ref, halo, rows, chunks):
    i = pl.program_id(0)
    last = pl.num_programs(0) - 1
    width = buf.shape[1]
    buf[halo:halo + rows, :] = main_ref[...]
    piece = lax.broadcasted_iota(jnp.int32, (halo, width), 0) & (chunks - 1)

    @pl.when(i > 0)
    def _():
        buf[0:halo, :] = prev_ref[...]

    @pl.when(i == 0)
    def _():
        if chunks == 1:
            buf[0:halo, :] = jnp.zeros((halo, width), F32)
        else:
            rolled = pltpu.roll(prev_ref[...], 1, axis=0)
            buf[0:halo, :] = jnp.where(piece != 0, rolled, 0.0)

    @pl.when(i < last)
    def _():
        buf[halo + rows:, :] = next_ref[...]

    @pl.when(i == last)
    def _():
        if chunks == 1:
            buf[halo + rows:, :] = jnp.zeros((halo, width), F32)
        else:
            rolled = pltpu.roll(next_ref[...], halo - 1, axis=0)
            buf[halo + rows:, :] = jnp.where(piece != chunks - 1, rolled, 0.0)


def _halo_specs(rows, halo, width, n_rows_total):
    per = rows // halo
    nblk = n_rows_total // halo
    return [
        pl.BlockSpec((halo, width), lambda i: ((i * per + nblk - 1) % nblk, 0)),
        pl.BlockSpec((rows, width), lambda i: (i, 0)),
        pl.BlockSpec((halo, width), lambda i: (((i + 1) * per) % nblk, 0)),
    ]


CONV_ROW_CHUNK = 32


def _mixer_body(gp_ref, g_ref, gn_ref, rp_ref, r_ref, rn_ref,
                cw_ref, cb_ref, clg_ref, clb_ref, rw_ref, rb_ref,
                wg_ref, ba_ref, bx_ref, lam_ref,
                c_ref, hf_ref, pf_ref, hb_ref, pb_ref,
                gbuf, rbuf, a_buf, b_buf, *, rows, chunks):
    halo = CONV_HALO_ROWS
    _fill_with_halo(gbuf, gp_ref, g_ref, gn_ref, halo, rows, chunks)
    _fill_with_halo(rbuf, rp_ref, r_ref, rn_ref, halo, rows, chunks)

    half = CONV_WIDTH // 2

    def conv_chunk(j, _):
        base = pl.multiple_of(j * CONV_ROW_CHUNK, CONV_ROW_CHUNK)
        acc = jnp.broadcast_to(cb_ref[...], (CONV_ROW_CHUNK, D_CONV))
        for k in range(CONV_WIDTH):
            start = base + (halo + (k - half) * NV)
            acc = acc + gbuf[pl.ds(start, CONV_ROW_CHUNK), :] * cw_ref[k:k + 1, :]
        c_ref[pl.ds(base, CONV_ROW_CHUNK), :] = acc
        return 0

    lax.fori_loop(0, rows // CONV_ROW_CHUNK, conv_chunk, 0)
    c_ref[...] = jax.nn.silu(_ln(c_ref[...], clg_ref[...], clb_ref[...]))

    nsteps = rows // NV
    for d in range(2):
        x = jnp.broadcast_to(rb_ref[d:d + 1, :], (rows, D_RNN))
        for k in range(RNN_CONV_WIDTH):
            shift = (k - (RNN_CONV_WIDTH - 1)) if d == 0 else k
            start = halo + shift * NV
            x = x + rbuf[start:start + rows, :] * rw_ref[d * RNN_CONV_WIDTH + k:
                                                         d * RNN_CONV_WIDTH + k + 1, :]
        xb = x.astype(BF16)
        half_w = D_RNN // 2
        parts = [jnp.dot(xb[:, hf * half_w:(hf + 1) * half_w], wg_ref[d, hf],
                         preferred_element_type=F32) for hf in range(2)]
        gate_a = jnp.concatenate([p[:, :half_w] for p in parts], axis=-1) + ba_ref[d:d + 1, :]
        gate_x = jnp.concatenate([p[:, half_w:] for p in parts], axis=-1) + bx_ref[d:d + 1, :]
        z = -lam_ref[d:d + 1, :]
        softplus = jnp.maximum(z, 0.0) + jnp.log1p(jnp.exp(-jnp.abs(z)))
        log_a = (-LRU_C) * jax.nn.sigmoid(gate_a) * softplus
        a = jnp.exp(log_a)
        one_minus_a2 = -jnp.tanh(log_a) * (a * a + 1.0)
        b = jnp.sqrt(one_minus_a2) * (jax.nn.sigmoid(gate_x) * x)
        a_buf[d] = a
        b_buf[d] = b

    def scan_step(t, carry):
        hf, pf, hb, pb = carry
        rf = pl.multiple_of(t * NV, NV)
        rb = pl.multiple_of((nsteps - 1 - t) * NV, NV)
        af = a_buf[0, pl.ds(rf, NV), :]
        hf = af * hf + b_buf[0, pl.ds(rf, NV), :]
        pf = af * pf
        hf_ref[pl.ds(rf, NV), :] = hf
        pf_ref[pl.ds(rf, NV), :] = pf
        ab = a_buf[1, pl.ds(rb, NV), :]
        hb = ab * hb + b_buf[1, pl.ds(rb, NV), :]
        pb = ab * pb
        hb_ref[pl.ds(rb, NV), :] = hb
        pb_ref[pl.ds(rb, NV), :] = pb
        return hf, pf, hb, pb

    zeros = jnp.zeros((NV, D_RNN), F32)
    ones = jnp.ones((NV, D_RNN), F32)
    lax.fori_loop(0, nsteps, scan_step, (zeros, ones, zeros, ones), unroll=4)


def _mixer(g, rx, p, chunks):
    rows_total = g.shape[0]
    rows = TILE_ROWS
    halo = CONV_HALO_ROWS
    small = [p["conv_w"], p["conv_b"], p["conv_ln_g"], p["conv_ln_b"],
             p["rnn_conv_w"], p["rnn_conv_b"], p["w_gate"], p["b_a"], p["b_x"], p["lam"]]
    out_spec = pl.BlockSpec((rows, D_RNN), lambda i: (i, 0))
    return pl.pallas_call(
        functools.partial(_mixer_body, rows=rows, chunks=chunks),
        grid=(rows_total // rows,),
        in_specs=(_halo_specs(rows, halo, D_CONV, rows_total)
                  + _halo_specs(rows, halo, D_RNN, rows_total)
                  + [_const_spec(a.shape) for a in small]),
        out_specs=[out_spec] * 5,
        out_shape=[jax.ShapeDtypeStruct((rows_total, D_RNN), F32)] * 5,
        scratch_shapes=[
            pltpu.VMEM((rows + 2 * halo, D_CONV), F32),
            pltpu.VMEM((rows + 2 * halo, D_RNN), F32),
            pltpu.VMEM((2, rows, D_RNN), F32),
            pltpu.VMEM((2, rows, D_RNN), F32),
        ],
        compiler_params=_params(1),
        name="mixer",
    )(g, g, g, rx, rx, rx, *small)


def _carry_body(hfe_ref, pfe_ref, hbe_ref, pbe_ref,
                cf_ref, qf_ref, cb_ref, qb_ref, ef_ref, eb_ref,
                sf, uf, sb, ub):
    @pl.when(pl.program_id(0) == 0)
    def _():
        sf[...] = jnp.zeros_like(sf)
        sb[...] = jnp.zeros_like(sb)
        uf[...] = jnp.ones_like(uf)
        ub[...] = jnp.ones_like(ub)

    cf_ref[...] = sf[...]
    qf_ref[...] = uf[...]
    cb_ref[...] = sb[...]
    qb_ref[...] = ub[...]
    sf[...] = hfe_ref[...] + pfe_ref[...] * sf[...]
    uf[...] = pfe_ref[...] * uf[...]
    sb[...] = hbe_ref[...] + pbe_ref[...] * sb[...]
    ub[...] = pbe_ref[...] * ub[...]
    ef_ref[...] = sf[...]
    eb_ref[...] = sb[...]


def _carry(hf, pf, hb, pb):
    rows_total = hf.shape[0]
    nt = rows_total // TILE_ROWS
    per = TILE_ROWS // NV
    blk = (NV, D_RNN)
    fwd_end = pl.BlockSpec(blk, lambda i: ((i + 1) * per - 1, 0))
    bwd_end = pl.BlockSpec(blk, lambda i: ((nt - 1 - i) * per, 0))
    fwd_out = pl.BlockSpec(blk, lambda i: (i, 0))
    bwd_out = pl.BlockSpec(blk, lambda i: (nt - 1 - i, 0))
    return pl.pallas_call(
        _carry_body,
        grid=(nt,),
        in_specs=[fwd_end, fwd_end, bwd_end, bwd_end],
        out_specs=[fwd_out, fwd_out, bwd_out, bwd_out,
                   pl.BlockSpec(blk, lambda i: (0, 0)), pl.BlockSpec(blk, lambda i: (0, 0))],
        out_shape=[jax.ShapeDtypeStruct((nt * NV, D_RNN), F32)] * 4
        + [jax.ShapeDtypeStruct(blk, F32)] * 2,
        scratch_shapes=[pltpu.VMEM(blk, F32)] * 4,
        compiler_params=_params(1),
        name="carry",
    )(hf, pf, hb, pb)


def _outproj_body(x_ref, c_ref, hf_ref, pf_ref, hb_ref, pb_ref, gr_ref,
                  cf_ref, qf_ref, cb_ref, qb_ref, ef_ref, eb_ref,
                  wo_ref, lg_ref, lb_ref, o_ref, *, rows, chunks):
    cf = cf_ref[...]
    cb = cb_ref[...]
    if chunks > 1:
        piece = lax.broadcasted_iota(jnp.int32, (NV, D_RNN), 0) & (chunks - 1)
        ef = jnp.where(piece != 0, pltpu.roll(ef_ref[...], 1, axis=0), 0.0)
        eb = jnp.where(piece != chunks - 1, pltpu.roll(eb_ref[...], NV - 1, axis=0), 0.0)
        cf = cf + qf_ref[...] * ef
        cb = cb + qb_ref[...] * eb
    nsteps = rows // NV
    shape3 = (nsteps, NV, D_RNN)
    rec = (hf_ref[...].reshape(shape3) + pf_ref[...].reshape(shape3) * cf[None]
           + hb_ref[...].reshape(shape3) + pb_ref[...].reshape(shape3) * cb[None])
    rec = rec.reshape(rows, D_RNN) * gr_ref[...]
    mix = (jnp.dot(c_ref[...].astype(BF16), wo_ref[0:D_CONV, :], preferred_element_type=F32)
           + jnp.dot(rec.astype(BF16), wo_ref[D_CONV:, :], preferred_element_type=F32))
    o_ref[...] = _ln(DEEPNORM_ALPHA * x_ref[...] + mix, lg_ref[...], lb_ref[...])


def _outproj(x, c, hf, pf, hb, pb, gr, carries, p, chunks):
    rows_total = x.shape[0]
    rows = TILE_ROWS
    cf, qf, cb, qb, ef, eb = carries
    blk = (NV, D_RNN)
    wide = pl.BlockSpec((rows, D_MODEL), lambda i: (i, 0))
    narrow = pl.BlockSpec((rows, D_RNN), lambda i: (i, 0))
    tile_state = pl.BlockSpec(blk, lambda i: (i, 0))
    return pl.pallas_call(
        functools.partial(_outproj_body, rows=rows, chunks=chunks),
        grid=(rows_total // rows,),
        in_specs=[wide] + [narrow] * 6 + [tile_state] * 4 + [_const_spec(blk)] * 2
        + [_const_spec((D_MODEL, D_MODEL)), _const_spec((1, D_MODEL)), _const_spec((1, D_MODEL))],
        out_specs=wide,
        out_shape=jax.ShapeDtypeStruct((rows_total, D_MODEL), F32),
        compiler_params=_params(1),
        name="outproj",
    )(x, c, hf, pf, hb, pb, gr, cf, qf, cb, qb, ef, eb, p["w_out"], p["ln1_g"], p["ln1_b"])


def _ffn_body(xp_ref, x_ref, xn_ref, wu_ref, dw_ref, db_ref, wd_ref, lg_ref, lb_ref,
              o_ref, xbuf, pbuf, *, rows, chunks):
    halo = FFN_HALO_ROWS
    _fill_with_halo(xbuf, xp_ref, x_ref, xn_ref, halo, rows, chunks)
    xe = xbuf[...].astype(BF16)

    def conv3(u, idx):
        w = dw_ref[idx]
        return (u[0:rows] * w[0:1, :] + u[NV:NV + rows] * w[1:2, :]
                + u[2 * NV:2 * NV + rows] * w[2:3, :] + db_ref[idx])

    for j in range(N_FFN_CHUNKS):
        uv = jnp.dot(xe, wu_ref[j], preferred_element_type=F32)
        ug = jnp.dot(xe, wu_ref[N_FFN_CHUNKS + j], preferred_element_type=F32)
        val = conv3(uv, j)
        gate = conv3(ug, N_FFN_CHUNKS + j)
        pbuf[:, j * FFN_CHUNK:(j + 1) * FFN_CHUNK] = (jax.nn.gelu(gate) * val).astype(BF16)
    f = jnp.dot(pbuf[...], wd_ref[...], preferred_element_type=F32)
    o_ref[...] = _ln(DEEPNORM_ALPHA * x_ref[...] + f, lg_ref[...], lb_ref[...])


def _ffn(x, p, chunks):
    rows_total = x.shape[0]
    rows = TILE_ROWS
    small = [p["w_up"], p["ffn_dw_w"], p["ffn_dw_b"], p["w_down"], p["ln2_g"], p["ln2_b"]]
    return pl.pallas_call(
        functools.partial(_ffn_body, rows=rows, chunks=chunks),
        grid=(rows_total // rows,),
        in_specs=_halo_specs(rows, FFN_HALO_ROWS, D_MODEL, rows_total)
        + [_const_spec(a.shape) for a in small],
        out_specs=pl.BlockSpec((rows, D_MODEL), lambda i: (i, 0)),
        out_shape=jax.ShapeDtypeStruct((rows_total, D_MODEL), F32),
        scratch_shapes=[
            pltpu.VMEM((rows + 2 * FFN_HALO_ROWS, D_MODEL), F32),
            pltpu.VMEM((rows, D_FF), BF16),
        ],
        compiler_params=_params(1),
        name="ffn",
    )(x, x, x, *small)


def _block_diag_gate(w_a, w_x):
    heads_per_half = RNN_HEADS // 2
    half_w = D_RNN // 2

    def bd(w):
        eye = jnp.eye(heads_per_half, dtype=w.dtype)
        return jnp.einsum("hij,hg->higj", w, eye).reshape(half_w, half_w)

    out = []
    for d in range(2):
        halves = []
        for hf in range(2):
            sl = slice(hf * heads_per_half, (hf + 1) * heads_per_half)
            halves.append(jnp.concatenate([bd(w_a[d, sl]), bd(w_x[d, sl])], axis=-1))
        out.append(jnp.stack(halves))
    return jnp.stack(out).astype(BF16)


def _layer_params(l, w_in, conv_dw_w, conv_dw_b, conv_ln_g, conv_ln_b, rnn_conv_w, rnn_conv_b,
                  rg_w_a, rg_b_a, rg_w_x, rg_b_x, rg_lambda, w_out, ln1_g, ln1_b, w_up,
                  ffn_dw_w, ffn_dw_b, w_down, ln2_g, ln2_b):
    nc = N_FFN_CHUNKS
    row = lambda a: a.reshape(1, -1)
    return {
        "w_in": w_in[l].astype(BF16),
        "conv_w": conv_dw_w[l], "conv_b": row(conv_dw_b[l]),
        "conv_ln_g": row(conv_ln_g[l]), "conv_ln_b": row(conv_ln_b[l]),
        "rnn_conv_w": rnn_conv_w[l].reshape(2 * RNN_CONV_WIDTH, D_RNN),
        "rnn_conv_b": rnn_conv_b[l],
        "w_gate": _block_diag_gate(rg_w_a[l], rg_w_x[l]),
        "b_a": rg_b_a[l], "b_x": rg_b_x[l], "lam": rg_lambda[l],
        "w_out": w_out[l].astype(BF16), "ln1_g": row(ln1_g[l]), "ln1_b": row(ln1_b[l]),
        "w_up": w_up[l].astype(BF16).reshape(D_MODEL, 2 * nc, FFN_CHUNK).transpose(1, 0, 2),
        "ffn_dw_w": ffn_dw_w[l].reshape(3, 2 * nc, FFN_CHUNK).transpose(1, 0, 2),
        "ffn_dw_b": ffn_dw_b[l].reshape(2 * nc, 1, FFN_CHUNK),
        "w_down": w_down[l].astype(BF16),
        "ln2_g": row(ln2_g[l]), "ln2_b": row(ln2_b[l]),
    }


def _trunk(x, ln_in_g, ln_in_b, layers):
    bsz, seq, _ = x.shape
    chunks = NV // bsz
    sv = seq // chunks
    xt = x.reshape(NV, sv, D_MODEL).transpose(1, 0, 2).reshape(sv * NV, D_MODEL)
    ln_g = ln_in_g.reshape(1, -1)
    ln_b = ln_in_b.reshape(1, -1)
    for l, p in enumerate(layers):
        if l == 0:
            xt, g, rx, gr = _inproj(xt, ln_g, ln_b, p["w_in"], first=True)
        else:
            g, rx, gr = _inproj(xt, ln_g, ln_b, p["w_in"], first=False)
        c, hf, pf, hb, pb = _mixer(g, rx, p, chunks)
        carries = _carry(hf, pf, hb, pb)
        x1 = _outproj(xt, c, hf, pf, hb, pb, gr, carries, p, chunks)
        xt = _ffn(x1, p, chunks)
    return xt.reshape(sv, NV, D_MODEL).transpose(1, 0, 2).reshape(bsz, seq, D_MODEL)


def kernel(x_prompt, x_sample, ln_in_g, ln_in_b, w_in, conv_dw_w, conv_dw_b, conv_ln_g,
           conv_ln_b, rnn_conv_w, rnn_conv_b, rg_w_a, rg_b_a, rg_w_x, rg_b_x, rg_lambda,
           w_out, ln1_g, ln1_b, w_up, ffn_dw_w, ffn_dw_b, w_down, ln2_g, ln2_b):
    layers = [
        _layer_params(l, w_in, conv_dw_w, conv_dw_b, conv_ln_g, conv_ln_b, rnn_conv_w,
                      rnn_conv_b, rg_w_a, rg_b_a, rg_w_x, rg_b_x, rg_lambda, w_out, ln1_g,
                      ln1_b, w_up, ffn_dw_w, ffn_dw_b, w_down, ln2_g, ln2_b)
        for l in range(DEPTH)
    ]
    y_prompt = _trunk(x_prompt, ln_in_g, ln_in_b, layers)
    y_sample = _trunk(x_sample, ln_in_g, ln_in_b, layers)
    return (y_prompt, y_sample)
```

```python
import functools

import jax
import jax.numpy as jnp
from jax import lax
from jax.experimental import pallas as pl
from jax.experimental.pallas import tpu as pltpu

D_MODEL = 1024
D_CONV = 512
D_RNN = 512
D_IN = 2 * D_CONV + 2 * D_RNN
CONV_WIDTH = 31
RNN_HEADS = 8
RNN_CONV_WIDTH = 4
LRU_C = 8.0
D_FF = 2816
FFN_CHUNK = 256
N_FFN_CHUNKS = D_FF // FFN_CHUNK
LN_EPS = 1e-5
DEPTH = 2
DEEPNORM_ALPHA = (2.0 * DEPTH) ** 0.25

NV = 8
CONV_HALO_ROWS = 128
RNN_HALO_ROWS = 32
FFN_HALO_ROWS = NV
TILE_ROWS = 512
CONV_ROW_CHUNK = 32
VMEM_LIMIT = 56 * 1024 * 1024

F32 = jnp.float32
BF16 = jnp.bfloat16


def _ln(x, g, b):
    mu = jnp.mean(x, axis=-1, keepdims=True)
    xc = x - mu
    var = jnp.mean(xc * xc, axis=-1, keepdims=True)
    return xc * lax.rsqrt(var + LN_EPS) * g + b


def _const_spec(shape):
    nd = len(shape)
    return pl.BlockSpec(shape, lambda *_: (0,) * nd, pipeline_mode=pl.Buffered(1))


def _params():
    return pltpu.CompilerParams(
        dimension_semantics=("arbitrary",), vmem_limit_bytes=VMEM_LIMIT)


def _edge_halos(prev, nxt, chunks):
    halo = prev.shape[0]
    if chunks == 1:
        return jnp.zeros_like(prev), jnp.zeros_like(nxt)
    piece = lax.broadcasted_iota(jnp.int32, prev.shape, 0) & (chunks - 1)
    prev_edge = jnp.where(piece != 0, pltpu.roll(prev, 1, axis=0), 0.0)
    next_edge = jnp.where(piece != chunks - 1, pltpu.roll(nxt, halo - 1, axis=0), 0.0)
    return prev_edge, next_edge


def _fill_with_halo(buf, prev, main, nxt, chunks):
    halo = prev.shape[0]
    rows = main.shape[0]
    i = pl.program_id(0)
    last = pl.num_programs(0) - 1
    prev_edge, next_edge = _edge_halos(prev, nxt, chunks)
    buf[0:halo, :] = jnp.where(i == 0, prev_edge, prev)
    buf[halo:halo + rows, :] = main
    buf[halo + rows:, :] = jnp.where(i == last, next_edge, nxt)


def _halo_specs(rows, halo, width, n_rows_total):
    per = rows // halo
    nblk = n_rows_total // halo
    return [
        pl.BlockSpec((halo, width), lambda i: ((i * per + nblk - 1) % nblk, 0)),
        pl.BlockSpec((rows, width), lambda i: (i, 0)),
        pl.BlockSpec((halo, width), lambda i: (((i + 1) * per) % nblk, 0)),
    ]


def _proj_rnn_body(*refs, rows, chunks, first):
    (xp_ref, x_ref, xn_ref, lg_ref, lb_ref, w_ref,
     rw_ref, rb_ref, wg_ref, ba_ref, bx_ref, lam_ref) = refs[:12]
    outs = list(refs[12:])
    xo_ref = outs.pop(0) if first else None
    g_ref, gr_ref, s_ref, pb_ref, hb0_ref, pb0_ref = outs[:6]
    outs = outs[6:]
    if chunks > 1:
        qf_ref, ef_ref = outs[:2]
        outs = outs[2:]
    xbuf, rbuf, a_buf, b_buf, sf = outs[:5]
    sq = outs[5] if chunks > 1 else None

    halo = RNN_HALO_ROWS
    i = pl.program_id(0)

    xbuf[0:halo, :] = xp_ref[...]
    xbuf[halo:halo + rows, :] = x_ref[...]
    xbuf[halo + rows:, :] = xn_ref[...]
    x = xbuf[...]
    if first:
        x = _ln(x, lg_ref[...], lb_ref[...])
        xo_ref[...] = x[halo:halo + rows]
    h = jnp.dot(x.astype(BF16), w_ref[...], preferred_element_type=F32)
    hm = h[halo:halo + rows]
    g_ref[...] = hm[:, :D_CONV] * jax.nn.sigmoid(hm[:, D_CONV:2 * D_CONV])
    gr_ref[...] = jax.nn.gelu(hm[:, 2 * D_CONV + D_RNN:])
    rx = h[:, 2 * D_CONV:2 * D_CONV + D_RNN]
    _fill_with_halo(rbuf, rx[0:halo], rx[halo:halo + rows], rx[halo + rows:], chunks)

    half_w = D_RNN // 2
    for d in range(2):
        xd = jnp.broadcast_to(rb_ref[d:d + 1, :], (rows, D_RNN))
        for k in range(RNN_CONV_WIDTH):
            shift = (k - (RNN_CONV_WIDTH - 1)) if d == 0 else k
            start = halo + shift * NV
            tap = d * RNN_CONV_WIDTH + k
            xd = xd + rbuf[start:start + rows, :] * rw_ref[tap:tap + 1, :]
        xb = xd.astype(BF16)
        parts = [jnp.dot(xb[:, hf * half_w:(hf + 1) * half_w], wg_ref[d, hf],
                         preferred_element_type=F32) for hf in range(2)]
        gate_a = jnp.concatenate([p[:, :half_w] for p in parts], axis=-1) + ba_ref[d:d + 1, :]
        gate_x = jnp.concatenate([p[:, half_w:] for p in parts], axis=-1) + bx_ref[d:d + 1, :]
        z = -lam_ref[d:d + 1, :]
        softplus = jnp.maximum(z, 0.0) + jnp.log1p(jnp.exp(-jnp.abs(z)))
        log_a = (-LRU_C) * jax.nn.sigmoid(gate_a) * softplus
        a = jnp.exp(log_a)
        one_minus_a2 = -jnp.tanh(log_a) * (a * a + 1.0)
        root = jnp.where(one_minus_a2 > 0.0, one_minus_a2 * lax.rsqrt(one_minus_a2), 0.0)
        a_buf[d] = a
        b_buf[d] = root * (jax.nn.sigmoid(gate_x) * xd)

    @pl.when(i == 0)
    def _():
        sf[...] = jnp.zeros_like(sf)
        if chunks > 1:
            sq[...] = jnp.ones_like(sq)

    nsteps = rows // NV

    def make_step(accumulate):
        def step(t, carry):
            hf, qf, hb, pb = carry
            rf = pl.multiple_of(t * NV, NV)
            rb = pl.multiple_of((nsteps - 1 - t) * NV, NV)
            af = a_buf[0, pl.ds(rf, NV), :]
            hf = af * hf + b_buf[0, pl.ds(rf, NV), :]
            ab = a_buf[1, pl.ds(rb, NV), :]
            hb = ab * hb + b_buf[1, pl.ds(rb, NV), :]
            pb = ab * pb
            pb_ref[pl.ds(rb, NV), :] = pb
            if chunks > 1:
                qf = af * qf
                qf_ref[pl.ds(rf, NV), :] = qf
            if accumulate:
                s_ref[pl.ds(rf, NV), :] += hf
                s_ref[pl.ds(rb, NV), :] += hb
            else:
                s_ref[pl.ds(rf, NV), :] = hf
                s_ref[pl.ds(rb, NV), :] = hb
            return hf, qf, hb, pb
        return step

    zeros = jnp.zeros((NV, D_RNN), F32)
    ones = jnp.ones((NV, D_RNN), F32)
    q0 = sq[...] if chunks > 1 else ones
    carry = (sf[...], q0, zeros, ones)
    carry = lax.fori_loop(0, nsteps // 2, make_step(False), carry, unroll=4)
    hf, qf, hb, pb = lax.fori_loop(nsteps // 2, nsteps, make_step(True), carry, unroll=4)
    sf[...] = hf
    hb0_ref[...] = hb
    pb0_ref[...] = pb
    if chunks > 1:
        sq[...] = qf
        ef_ref[...] = hf


def _proj_rnn(x, ln_g, ln_b, p, chunks, first):
    rows_total = x.shape[0]
    rows = TILE_ROWS
    nt = rows_total // rows
    halo = RNN_HALO_ROWS
    small = [ln_g, ln_b, p["w_in"], p["rnn_conv_w"], p["rnn_conv_b"], p["w_gate"],
             p["b_a"], p["b_x"], p["lam"]]
    blk = (NV, D_RNN)
    wide = pl.BlockSpec((rows, D_MODEL), lambda i: (i, 0))
    narrow = pl.BlockSpec((rows, D_RNN), lambda i: (i, 0))
    per_tile = pl.BlockSpec(blk, lambda i: (i, 0))
    big = jax.ShapeDtypeStruct((rows_total, D_RNN), F32)
    summary = jax.ShapeDtypeStruct((nt * NV, D_RNN), F32)
    out_specs = ([wide] if first else []) + [narrow] * 4 + [per_tile] * 2
    out_shape = ([jax.ShapeDtypeStruct((rows_total, D_MODEL), F32)] if first else []) \
        + [big] * 4 + [summary] * 2
    scratch = [
        pltpu.VMEM((rows + 2 * halo, D_MODEL), F32),
        pltpu.VMEM((rows + 2 * halo, D_RNN), F32),
        pltpu.VMEM((2, rows, D_RNN), F32),
        pltpu.VMEM((2, rows, D_RNN), F32),
        pltpu.VMEM(blk, F32),
    ]
    if chunks > 1:
        out_specs += [narrow, pl.BlockSpec(blk, lambda i: (0, 0))]
        out_shape += [big, jax.ShapeDtypeStruct(blk, F32)]
        scratch.append(pltpu.VMEM(blk, F32))
    return pl.pallas_call(
        functools.partial(_proj_rnn_body, rows=rows, chunks=chunks, first=first),
        grid=(nt,),
        in_specs=_halo_specs(rows, halo, D_MODEL, rows_total)
        + [_const_spec(a.shape) for a in small],
        out_specs=out_specs,
        out_shape=out_shape,
        scratch_shapes=scratch,
        compiler_params=_params(),
        name="proj_rnn",
    )(x, x, x, *small)


def _carry_body(hb0_ref, pb0_ref, cb_ref, qb_ref, eb_ref, sb, ub):
    @pl.when(pl.program_id(0) == 0)
    def _():
        sb[...] = jnp.zeros_like(sb)
        ub[...] = jnp.ones_like(ub)

    cb_ref[...] = sb[...]
    qb_ref[...] = ub[...]
    sb[...] = hb0_ref[...] + pb0_ref[...] * sb[...]
    ub[...] = pb0_ref[...] * ub[...]
    eb_ref[...] = sb[...]


def _carry(hb0, pb0):
    nt = hb0.shape[0] // NV
    blk = (NV, D_RNN)
    rev = pl.BlockSpec(blk, lambda i: (nt - 1 - i, 0))
    return pl.pallas_call(
        _carry_body,
        grid=(nt,),
        in_specs=[rev, rev],
        out_specs=[rev, rev, pl.BlockSpec(blk, lambda i: (0, 0))],
        out_shape=[jax.ShapeDtypeStruct((nt * NV, D_RNN), F32)] * 2
        + [jax.ShapeDtypeStruct(blk, F32)],
        scratch_shapes=[pltpu.VMEM(blk, F32)] * 2,
        compiler_params=_params(),
        name="carry",
    )(hb0, pb0)


def _mix_out_body(*refs, rows, chunks):
    (gp_ref, g_ref, gn_ref, x_ref, s_ref, pb_ref, gr_ref, cb_ref) = refs[:8]
    refs = refs[8:]
    if chunks > 1:
        qf_ref, qb_ref, ef_ref, eb_ref = refs[:4]
        refs = refs[4:]
    (cw_ref, cb_bias_ref, clg_ref, clb_ref, wo_ref, lg_ref, lb_ref,
     o_ref, gbuf, cbuf) = refs

    halo = CONV_HALO_ROWS
    _fill_with_halo(gbuf, gp_ref[...], g_ref[...], gn_ref[...], chunks)

    half = CONV_WIDTH // 2

    def conv_chunk(j, _):
        base = pl.multiple_of(j * CONV_ROW_CHUNK, CONV_ROW_CHUNK)
        acc = jnp.broadcast_to(cb_bias_ref[...], (CONV_ROW_CHUNK, D_CONV))
        for k in range(CONV_WIDTH):
            start = base + (halo + (k - half) * NV)
            acc = acc + gbuf[pl.ds(start, CONV_ROW_CHUNK), :] * cw_ref[k:k + 1, :]
        cbuf[pl.ds(base, CONV_ROW_CHUNK), :] = acc
        return 0

    lax.fori_loop(0, rows // CONV_ROW_CHUNK, conv_chunk, 0)
    c = jax.nn.silu(_ln(cbuf[...], clg_ref[...], clb_ref[...]))

    nsteps = rows // NV
    shape3 = (nsteps, NV, D_RNN)
    cb = cb_ref[...]
    rec = s_ref[...].reshape(shape3)
    if chunks > 1:
        ef, eb = _edge_halos(ef_ref[...], eb_ref[...], chunks)
        cb = cb + qb_ref[...] * eb
        rec = rec + qf_ref[...].reshape(shape3) * ef[None]
    rec = rec + pb_ref[...].reshape(shape3) * cb[None]
    rec = rec.reshape(rows, D_RNN) * gr_ref[...]

    mix = (jnp.dot(c.astype(BF16), wo_ref[0:D_CONV, :], preferred_element_type=F32)
           + jnp.dot(rec.astype(BF16), wo_ref[D_CONV:, :], preferred_element_type=F32))
    o_ref[...] = _ln(DEEPNORM_ALPHA * x_ref[...] + mix, lg_ref[...], lb_ref[...])


def _mix_out(x, g, gr, s, pb, cb, extra, p, chunks):
    rows_total = x.shape[0]
    rows = TILE_ROWS
    halo = CONV_HALO_ROWS
    blk = (NV, D_RNN)
    wide = pl.BlockSpec((rows, D_MODEL), lambda i: (i, 0))
    narrow = pl.BlockSpec((rows, D_RNN), lambda i: (i, 0))
    per_tile = pl.BlockSpec(blk, lambda i: (i, 0))
    small = [p["conv_w"], p["conv_b"], p["conv_ln_g"], p["conv_ln_b"],
             p["w_out"], p["ln1_g"], p["ln1_b"]]
    in_specs = _halo_specs(rows, halo, D_CONV, rows_total) + [wide] + [narrow] * 3 + [per_tile]
    args = [g, g, g, x, s, pb, gr, cb]
    if chunks > 1:
        qf, qb, ef, eb = extra
        in_specs += [narrow, per_tile, _const_spec(blk), _const_spec(blk)]
        args += [qf, qb, ef, eb]
    return pl.pallas_call(
        functools.partial(_mix_out_body, rows=rows, chunks=chunks),
        grid=(rows_total // rows,),
        in_specs=in_specs + [_const_spec(a.shape) for a in small],
        out_specs=wide,
        out_shape=jax.ShapeDtypeStruct((rows_total, D_MODEL), F32),
        scratch_shapes=[
            pltpu.VMEM((rows + 2 * halo, D_CONV), F32),
            pltpu.VMEM((rows, D_CONV), F32),
        ],
        compiler_params=_params(),
        name="mix_out",
    )(*args, *small)


def _ffn_body(xp_ref, x_ref, xn_ref, wu_ref, dw_ref, db_ref, wd_ref, lg_ref, lb_ref,
              o_ref, xbuf, pbuf, *, rows, chunks):
    _fill_with_halo(xbuf, xp_ref[...], x_ref[...], xn_ref[...], chunks)
    xe = xbuf[...].astype(BF16)

    def conv3(u, idx):
        w = dw_ref[idx]
        return (u[0:rows] * w[0:1, :] + u[NV:NV + rows] * w[1:2, :]
                + u[2 * NV:2 * NV + rows] * w[2:3, :] + db_ref[idx])

    for j in range(N_FFN_CHUNKS):
        uv = jnp.dot(xe, wu_ref[j], preferred_element_type=F32)
        ug = jnp.dot(xe, wu_ref[N_FFN_CHUNKS + j], preferred_element_type=F32)
        val = conv3(uv, j)
        gate = conv3(ug, N_FFN_CHUNKS + j)
        pbuf[:, j * FFN_CHUNK:(j + 1) * FFN_CHUNK] = (jax.nn.gelu(gate) * val).astype(BF16)
    f = jnp.dot(pbuf[...], wd_ref[...], preferred_element_type=F32)
    o_ref[...] = _ln(DEEPNORM_ALPHA * x_ref[...] + f, lg_ref[...], lb_ref[...])


def _ffn(x, p, chunks):
    rows_total = x.shape[0]
    rows = TILE_ROWS
    small = [p["w_up"], p["ffn_dw_w"], p["ffn_dw_b"], p["w_down"], p["ln2_g"], p["ln2_b"]]
    return pl.pallas_call(
        functools.partial(_ffn_body, rows=rows, chunks=chunks),
        grid=(rows_total // rows,),
        in_specs=_halo_specs(rows, FFN_HALO_ROWS, D_MODEL, rows_total)
        + [_const_spec(a.shape) for a in small],
        out_specs=pl.BlockSpec((rows, D_MODEL), lambda i: (i, 0)),
        out_shape=jax.ShapeDtypeStruct((rows_total, D_MODEL), F32),
        scratch_shapes=[
            pltpu.VMEM((rows + 2 * FFN_HALO_ROWS, D_MODEL), F32),
            pltpu.VMEM((rows, D_FF), BF16),
        ],
        compiler_params=_params(),
        name="ffn",
    )(x, x, x, *small)


def _block_diag_gate(w_a, w_x):
    heads_per_half = RNN_HEADS // 2
    half_w = D_RNN // 2

    def bd(w):
        eye = jnp.eye(heads_per_half, dtype=w.dtype)
        return jnp.einsum("hij,hg->higj", w, eye).reshape(half_w, half_w)

    out = []
    for d in range(2):
        halves = []
        for hf in range(2):
            sl = slice(hf * heads_per_half, (hf + 1) * heads_per_half)
            halves.append(jnp.concatenate([bd(w_a[d, sl]), bd(w_x[d, sl])], axis=-1))
        out.append(jnp.stack(halves))
    return jnp.stack(out).astype(BF16)


def _layer_params(l, w_in, conv_dw_w, conv_dw_b, conv_ln_g, conv_ln_b, rnn_conv_w, rnn_conv_b,
                  rg_w_a, rg_b_a, rg_w_x, rg_b_x, rg_lambda, w_out, ln1_g, ln1_b, w_up,
                  ffn_dw_w, ffn_dw_b, w_down, ln2_g, ln2_b):
    nc = N_FFN_CHUNKS
    row = lambda a: a.reshape(1, -1)
    return {
        "w_in": w_in[l].astype(BF16),
        "conv_w": conv_dw_w[l], "conv_b": row(conv_dw_b[l]),
        "conv_ln_g": row(conv_ln_g[l]), "conv_ln_b": row(conv_ln_b[l]),
        "rnn_conv_w": rnn_conv_w[l].reshape(2 * RNN_CONV_WIDTH, D_RNN),
        "rnn_conv_b": rnn_conv_b[l],
        "w_gate": _block_diag_gate(rg_w_a[l], rg_w_x[l]),
        "b_a": rg_b_a[l], "b_x": rg_b_x[l], "lam": rg_lambda[l],
        "w_out": w_out[l].astype(BF16), "ln1_g": row(ln1_g[l]), "ln1_b": row(ln1_b[l]),
        "w_up": w_up[l].astype(BF16).reshape(D_MODEL, 2 * nc, FFN_CHUNK).transpose(1, 0, 2),
        "ffn_dw_w": ffn_dw_w[l].reshape(3, 2 * nc, FFN_CHUNK).transpose(1, 0, 2),
        "ffn_dw_b": ffn_dw_b[l].reshape(2 * nc, 1, FFN_CHUNK),
        "w_down": w_down[l].astype(BF16),
        "ln2_g": row(ln2_g[l]), "ln2_b": row(ln2_b[l]),
    }


def _trunk(x, ln_in_g, ln_in_b, layers):
    bsz, seq, _ = x.shape
    chunks = NV // bsz
    sv = seq // chunks
    xt = x.reshape(NV, sv, D_MODEL).transpose(1, 0, 2).reshape(sv * NV, D_MODEL)
    ln_g = ln_in_g.reshape(1, -1)
    ln_b = ln_in_b.reshape(1, -1)
    for l, p in enumerate(layers):
        outs = _proj_rnn(xt, ln_g, ln_b, p, chunks, first=(l == 0))
        if l == 0:
            xt, outs = outs[0], outs[1:]
        g, gr, s, pb, hb0, pb0 = outs[:6]
        cb, qb, eb = _carry(hb0, pb0)
        extra = (outs[6], qb, outs[7], eb) if chunks > 1 else None
        x1 = _mix_out(xt, g, gr, s, pb, cb, extra, p, chunks)
        xt = _ffn(x1, p, chunks)
    return xt.reshape(sv, NV, D_MODEL).transpose(1, 0, 2).reshape(bsz, seq, D_MODEL)


def kernel(x_prompt, x_sample, ln_in_g, ln_in_b, w_in, conv_dw_w, conv_dw_b, conv_ln_g,
           conv_ln_b, rnn_conv_w, rnn_conv_b, rg_w_a, rg_b_a, rg_w_x, rg_b_x, rg_lambda,
           w_out, ln1_g, ln1_b, w_up, ffn_dw_w, ffn_dw_b, w_down, ln2_g, ln2_b):
    layers = [
        _layer_params(l, w_in, conv_dw_w, conv_dw_b, conv_ln_g, conv_ln_b, rnn_conv_w,
                      rnn_conv_b, rg_w_a, rg_b_a, rg_w_x, rg_b_x, rg_lambda, w_out, ln1_g,
                      ln1_b, w_up, ffn_dw_w, ffn_dw_b, w_down, ln2_g, ln2_b)
        for l in range(DEPTH)
    ]
    y_prompt = _trunk(x_prompt, ln_in_g, ln_in_b, layers)
    y_sample = _trunk(x_sample, ln_in_g, ln_in_b, layers)
    return (y_prompt, y_sample)
```

```python
import functools

import jax
import jax.numpy as jnp
from jax import lax
from jax.experimental import pallas as pl
from jax.experimental.pallas import tpu as pltpu

D_MODEL = 1024
D_CONV = 512
D_RNN = 512
D_IN = 2 * D_CONV + 2 * D_RNN
CONV_WIDTH = 31
RNN_HEADS = 8
RNN_CONV_WIDTH = 4
LRU_C = 8.0
D_FF = 2816
FFN_CHUNK = 256
N_FFN_CHUNKS = D_FF // FFN_CHUNK
LN_EPS = 1e-5
DEPTH = 2
DEEPNORM_ALPHA = (2.0 * DEPTH) ** 0.25

NV = 8
CONV_HALO_ROWS = 128
RNN_HALO_ROWS = 32
FFN_HALO_ROWS = NV
TILE_ROWS = 512
TILE_STEPS = TILE_ROWS // NV
NAT_HALO_STEPS = 8
CONV_ROW_CHUNK = 64
LANES = 128
VMEM_LIMIT = 56 * 1024 * 1024

F32 = jnp.float32
BF16 = jnp.bfloat16


def _ln(x, g, b):
    mu = jnp.mean(x, axis=-1, keepdims=True)
    xc = x - mu
    var = jnp.mean(xc * xc, axis=-1, keepdims=True)
    return xc * lax.rsqrt(var + LN_EPS) * g + b


def _const_spec(shape):
    nd = len(shape)
    return pl.BlockSpec(shape, lambda *_: (0,) * nd, pipeline_mode=pl.Buffered(1))


def _params():
    return pltpu.CompilerParams(
        dimension_semantics=("arbitrary",), vmem_limit_bytes=VMEM_LIMIT)


def _to_time_major(x3):
    return pltpu.einshape("vtd->tvd", x3).reshape(x3.shape[1] * NV, x3.shape[2])


def _to_natural(x2):
    return pltpu.einshape("tvd->vtd", x2.reshape(x2.shape[0] // NV, NV, x2.shape[1]))


def _edge_halos(prev, nxt, chunks):
    halo = prev.shape[0]
    if chunks == 1:
        return jnp.zeros_like(prev), jnp.zeros_like(nxt)
    piece = lax.broadcasted_iota(jnp.int32, prev.shape, 0) & (chunks - 1)
    prev_edge = jnp.where(piece != 0, pltpu.roll(prev, 1, axis=0), 0.0)
    next_edge = jnp.where(piece != chunks - 1, pltpu.roll(nxt, halo - 1, axis=0), 0.0)
    return prev_edge, next_edge


def _fill_with_halo(buf, prev, main, nxt, chunks):
    halo = prev.shape[0]
    rows = main.shape[0]
    i = pl.program_id(0)
    last = pl.num_programs(0) - 1
    prev_edge, next_edge = _edge_halos(prev, nxt, chunks)
    buf[0:halo, :] = jnp.where(i == 0, prev_edge, prev)
    buf[halo:halo + rows, :] = main
    buf[halo + rows:, :] = jnp.where(i == last, next_edge, nxt)


def _halo_specs(rows, halo, width, n_rows_total):
    per = rows // halo
    nblk = n_rows_total // halo
    return [
        pl.BlockSpec((halo, width), lambda i: ((i * per + nblk - 1) % nblk, 0)),
        pl.BlockSpec((rows, width), lambda i: (i, 0)),
        pl.BlockSpec((halo, width), lambda i: (((i + 1) * per) % nblk, 0)),
    ]


def _natural_halo_specs(steps, width, n_steps_total):
    hs = NAT_HALO_STEPS
    per = steps // hs
    nblk = n_steps_total // hs
    return [
        pl.BlockSpec((NV, hs, width), lambda i: (0, (i * per + nblk - 1) % nblk, 0)),
        pl.BlockSpec((NV, steps, width), lambda i: (0, i, 0)),
        pl.BlockSpec((NV, hs, width), lambda i: (0, ((i + 1) * per) % nblk, 0)),
    ]


def _proj_rnn_body(*refs, rows, chunks, first):
    (xp_ref, x_ref, xn_ref, lg_ref, lb_ref, w_ref,
     rw_ref, rb_ref, wg_ref, ba_ref, bx_ref, lam_ref) = refs[:12]
    outs = list(refs[12:])
    xo_ref = outs.pop(0) if first else None
    g_ref, gr_ref, s_ref, pb_ref, hb0_ref, pb0_ref = outs[:6]
    outs = outs[6:]
    if chunks > 1:
        qf_ref, ef_ref = outs[:2]
        outs = outs[2:]
    xbuf, rbuf, a_buf, b_buf, sf = outs[:5]
    sq = outs[5] if chunks > 1 else None

    halo = RNN_HALO_ROWS
    i = pl.program_id(0)

    if first:
        nat_halo = NAT_HALO_STEPS * NV
        xbuf[0:halo, :] = _to_time_major(xp_ref[...])[nat_halo - halo:]
        xbuf[halo:halo + rows, :] = _to_time_major(x_ref[...])
        xbuf[halo + rows:, :] = _to_time_major(xn_ref[...])[0:halo]
        x = _ln(xbuf[...], lg_ref[...], lb_ref[...])
        xo_ref[...] = x[halo:halo + rows]
    else:
        xbuf[0:halo, :] = xp_ref[...]
        xbuf[halo:halo + rows, :] = x_ref[...]
        xbuf[halo + rows:, :] = xn_ref[...]
        x = xbuf[...]
    h = jnp.dot(x.astype(BF16), w_ref[...], preferred_element_type=F32)
    hm = h[halo:halo + rows]
    g_ref[...] = hm[:, :D_CONV] * jax.nn.sigmoid(hm[:, D_CONV:2 * D_CONV])
    gr_ref[...] = jax.nn.gelu(hm[:, 2 * D_CONV + D_RNN:])
    rx = h[:, 2 * D_CONV:2 * D_CONV + D_RNN]
    _fill_with_halo(rbuf, rx[0:halo], rx[halo:halo + rows], rx[halo + rows:], chunks)

    half_w = D_RNN // 2
    for d in range(2):
        xd = jnp.broadcast_to(rb_ref[d:d + 1, :], (rows, D_RNN))
        for k in range(RNN_CONV_WIDTH):
            shift = (k - (RNN_CONV_WIDTH - 1)) if d == 0 else k
            start = halo + shift * NV
            tap = d * RNN_CONV_WIDTH + k
            xd = xd + rbuf[start:start + rows, :] * rw_ref[tap:tap + 1, :]
        xb = xd.astype(BF16)
        parts = [jnp.dot(xb[:, hf * half_w:(hf + 1) * half_w], wg_ref[d, hf],
                         preferred_element_type=F32) for hf in range(2)]
        gate_a = jnp.concatenate([p[:, :half_w] for p in parts], axis=-1) + ba_ref[d:d + 1, :]
        gate_x = jnp.concatenate([p[:, half_w:] for p in parts], axis=-1) + bx_ref[d:d + 1, :]
        z = -lam_ref[d:d + 1, :]
        softplus = jnp.maximum(z, 0.0) + jnp.log1p(jnp.exp(-jnp.abs(z)))
        log_a = (-LRU_C) * jax.nn.sigmoid(gate_a) * softplus
        a = jnp.exp(log_a)
        one_minus_a2 = -jnp.tanh(log_a) * (a * a + 1.0)
        root = jnp.where(one_minus_a2 > 0.0, one_minus_a2 * lax.rsqrt(one_minus_a2), 0.0)
        a_buf[d] = a
        b_buf[d] = root * (jax.nn.sigmoid(gate_x) * xd)

    @pl.when(i == 0)
    def _():
        sf[...] = jnp.zeros_like(sf)
        if chunks > 1:
            sq[...] = jnp.ones_like(sq)

    nsteps = rows // NV

    def make_step(accumulate):
        def step(t, carry):
            hf, qf, hb, pb = carry
            rf = pl.multiple_of(t * NV, NV)
            rb = pl.multiple_of((nsteps - 1 - t) * NV, NV)
            af = a_buf[0, pl.ds(rf, NV), :]
            hf = af * hf + b_buf[0, pl.ds(rf, NV), :]
            ab = a_buf[1, pl.ds(rb, NV), :]
            hb = ab * hb + b_buf[1, pl.ds(rb, NV), :]
            pb = ab * pb
            pb_ref[pl.ds(rb, NV), :] = pb
            if chunks > 1:
                qf = af * qf
                qf_ref[pl.ds(rf, NV), :] = qf
            if accumulate:
                s_ref[pl.ds(rf, NV), :] += hf
                s_ref[pl.ds(rb, NV), :] += hb
            else:
                s_ref[pl.ds(rf, NV), :] = hf
                s_ref[pl.ds(rb, NV), :] = hb
            return hf, qf, hb, pb
        return step

    zeros = jnp.zeros((NV, D_RNN), F32)
    ones = jnp.ones((NV, D_RNN), F32)
    q0 = sq[...] if chunks > 1 else ones
    carry = (sf[...], q0, zeros, ones)
    carry = lax.fori_loop(0, nsteps // 2, make_step(False), carry, unroll=4)
    hf, qf, hb, pb = lax.fori_loop(nsteps // 2, nsteps, make_step(True), carry, unroll=4)
    sf[...] = hf
    hb0_ref[...] = hb
    pb0_ref[...] = pb
    if chunks > 1:
        sq[...] = qf
        ef_ref[...] = hf


def _proj_rnn(x, ln_g, ln_b, p, chunks, first):
    rows = TILE_ROWS
    halo = RNN_HALO_ROWS
    if first:
        rows_total = x.shape[1] * NV
        x_specs = _natural_halo_specs(TILE_STEPS, D_MODEL, x.shape[1])
    else:
        rows_total = x.shape[0]
        x_specs = _halo_specs(rows, halo, D_MODEL, rows_total)
    nt = rows_total // rows
    small = [ln_g, ln_b, p["w_in"], p["rnn_conv_w"], p["rnn_conv_b"], p["w_gate"],
             p["b_a"], p["b_x"], p["lam"]]
    blk = (NV, D_RNN)
    wide = pl.BlockSpec((rows, D_MODEL), lambda i: (i, 0))
    narrow = pl.BlockSpec((rows, D_RNN), lambda i: (i, 0))
    per_tile = pl.BlockSpec(blk, lambda i: (i, 0))
    big = jax.ShapeDtypeStruct((rows_total, D_RNN), F32)
    summary = jax.ShapeDtypeStruct((nt * NV, D_RNN), F32)
    out_specs = ([wide] if first else []) + [narrow] * 4 + [per_tile] * 2
    out_shape = ([jax.ShapeDtypeStruct((rows_total, D_MODEL), F32)] if first else []) \
        + [big] * 4 + [summary] * 2
    scratch = [
        pltpu.VMEM((rows + 2 * halo, D_MODEL), F32),
        pltpu.VMEM((rows + 2 * halo, D_RNN), F32),
        pltpu.VMEM((2, rows, D_RNN), F32),
        pltpu.VMEM((2, rows, D_RNN), F32),
        pltpu.VMEM(blk, F32),
    ]
    if chunks > 1:
        out_specs += [narrow, pl.BlockSpec(blk, lambda i: (0, 0))]
        out_shape += [big, jax.ShapeDtypeStruct(blk, F32)]
        scratch.append(pltpu.VMEM(blk, F32))
    return pl.pallas_call(
        functools.partial(_proj_rnn_body, rows=rows, chunks=chunks, first=first),
        grid=(nt,),
        in_specs=x_specs + [_const_spec(a.shape) for a in small],
        out_specs=out_specs,
        out_shape=out_shape,
        scratch_shapes=scratch,
        compiler_params=_params(),
        name="proj_rnn",
    )(x, x, x, *small)


def _carry_body(hb0_ref, pb0_ref, cb_ref, qb_ref, eb_ref, *, nt):
    def step(k, carry):
        sb, ub = carry
        r = pl.multiple_of((nt - 1 - k) * NV, NV)
        cb_ref[pl.ds(r, NV), :] = sb
        qb_ref[pl.ds(r, NV), :] = ub
        p0 = pb0_ref[pl.ds(r, NV), :]
        return hb0_ref[pl.ds(r, NV), :] + p0 * sb, p0 * ub

    init = (jnp.zeros((NV, D_RNN), F32), jnp.ones((NV, D_RNN), F32))
    sb, _ = lax.fori_loop(0, nt, step, init)
    eb_ref[...] = sb


def _carry(hb0, pb0):
    nt = hb0.shape[0] // NV
    return pl.pallas_call(
        functools.partial(_carry_body, nt=nt),
        out_shape=[jax.ShapeDtypeStruct((nt * NV, D_RNN), F32)] * 2
        + [jax.ShapeDtypeStruct((NV, D_RNN), F32)],
        compiler_params=pltpu.CompilerParams(vmem_limit_bytes=VMEM_LIMIT),
        name="carry",
    )(hb0, pb0)


def _mix_out_body(*refs, rows, chunks):
    (gp_ref, g_ref, gn_ref, x_ref, s_ref, pb_ref, gr_ref, cb_ref) = refs[:8]
    refs = refs[8:]
    if chunks > 1:
        qf_ref, qb_ref, ef_ref, eb_ref = refs[:4]
        refs = refs[4:]
    (cw_ref, cb_bias_ref, clg_ref, clb_ref, wo_ref, lg_ref, lb_ref,
     o_ref, gbuf, cbuf) = refs

    halo = CONV_HALO_ROWS
    i = pl.program_id(0)
    gp = gp_ref[...]
    gn = gn_ref[...]
    gp_edge, gn_edge = _edge_halos(gp, gn, chunks)
    gp = jnp.where(i == 0, gp_edge, gp)
    gn = jnp.where(i == pl.num_programs(0) - 1, gn_edge, gn)
    for col in range(D_CONV // LANES):
        lanes = slice(col * LANES, (col + 1) * LANES)
        gbuf[col, 0:halo, :] = gp[:, lanes]
        gbuf[col, halo:halo + rows, :] = g_ref[:, lanes]
        gbuf[col, halo + rows:, :] = gn[:, lanes]

    half = CONV_WIDTH // 2

    conv_shape = (CONV_ROW_CHUNK // NV, NV, LANES)
    for col in range(D_CONV // LANES):
        lanes = slice(col * LANES, (col + 1) * LANES)
        taps = [cw_ref[k * NV:(k + 1) * NV, lanes][None] for k in range(CONV_WIDTH)]
        bias = jnp.broadcast_to(cb_bias_ref[:, lanes][None], conv_shape)

        def conv_chunk(j, _, col=col, lanes=lanes, taps=taps, bias=bias):
            base = pl.multiple_of(j * CONV_ROW_CHUNK, CONV_ROW_CHUNK)
            acc = [bias, None]
            for k in range(CONV_WIDTH):
                start = base + (halo + (k - half) * NV)
                term = gbuf[col, pl.ds(start, CONV_ROW_CHUNK), :].reshape(conv_shape) * taps[k]
                acc[k % 2] = term if acc[k % 2] is None else acc[k % 2] + term
            total = acc[0] + acc[1]
            cbuf[pl.ds(base, CONV_ROW_CHUNK), lanes] = total.reshape(CONV_ROW_CHUNK, LANES)
            return 0

        lax.fori_loop(0, rows // CONV_ROW_CHUNK, conv_chunk, 0)
    c = jax.nn.silu(_ln(cbuf[...], clg_ref[...], clb_ref[...]))

    nsteps = rows // NV
    shape3 = (nsteps, NV, D_RNN)
    cb = cb_ref[...]
    rec = s_ref[...].reshape(shape3)
    if chunks > 1:
        ef, eb = _edge_halos(ef_ref[...], eb_ref[...], chunks)
        cb = cb + qb_ref[...] * eb
        rec = rec + qf_ref[...].reshape(shape3) * ef[None]
    rec = rec + pb_ref[...].reshape(shape3) * cb[None]
    rec = rec.reshape(rows, D_RNN) * gr_ref[...]

    mix = (jnp.dot(c.astype(BF16), wo_ref[0:D_CONV, :], preferred_element_type=F32)
           + jnp.dot(rec.astype(BF16), wo_ref[D_CONV:, :], preferred_element_type=F32))
    o_ref[...] = _ln(DEEPNORM_ALPHA * x_ref[...] + mix, lg_ref[...], lb_ref[...])


def _mix_out(x, g, gr, s, pb, cb, extra, p, chunks):
    rows_total = x.shape[0]
    rows = TILE_ROWS
    halo = CONV_HALO_ROWS
    blk = (NV, D_RNN)
    wide = pl.BlockSpec((rows, D_MODEL), lambda i: (i, 0))
    narrow = pl.BlockSpec((rows, D_RNN), lambda i: (i, 0))
    per_tile = pl.BlockSpec(blk, lambda i: (i, 0))
    small = [p["conv_w"], p["conv_b"], p["conv_ln_g"], p["conv_ln_b"],
             p["w_out"], p["ln1_g"], p["ln1_b"]]
    in_specs = _halo_specs(rows, halo, D_CONV, rows_total) + [wide] + [narrow] * 3 + [per_tile]
    args = [g, g, g, x, s, pb, gr, cb]
    if chunks > 1:
        qf, qb, ef, eb = extra
        in_specs += [narrow, per_tile, _const_spec(blk), _const_spec(blk)]
        args += [qf, qb, ef, eb]
    return pl.pallas_call(
        functools.partial(_mix_out_body, rows=rows, chunks=chunks),
        grid=(rows_total // rows,),
        in_specs=in_specs + [_const_spec(a.shape) for a in small],
        out_specs=wide,
        out_shape=jax.ShapeDtypeStruct((rows_total, D_MODEL), F32),
        scratch_shapes=[
            pltpu.VMEM((D_CONV // LANES, rows + 2 * halo, LANES), F32),
            pltpu.VMEM((rows, D_CONV), F32),
        ],
        compiler_params=_params(),
        name="mix_out",
    )(*args, *small)


def _ffn_body(xp_ref, x_ref, xn_ref, wu_ref, dw_ref, db_ref, wd_ref, lg_ref, lb_ref,
              o_ref, xbuf, pbuf, *, rows, chunks, last):
    _fill_with_halo(xbuf, xp_ref[...], x_ref[...], xn_ref[...], chunks)
    xe = xbuf[...].astype(BF16)

    def conv3(u, idx):
        w = dw_ref[idx]
        return (u[0:rows] * w[0:1, :] + u[NV:NV + rows] * w[1:2, :]
                + u[2 * NV:2 * NV + rows] * w[2:3, :] + db_ref[idx])

    for j in range(N_FFN_CHUNKS):
        uv = jnp.dot(xe, wu_ref[j], preferred_element_type=F32)
        ug = jnp.dot(xe, wu_ref[N_FFN_CHUNKS + j], preferred_element_type=F32)
        val = conv3(uv, j)
        gate = conv3(ug, N_FFN_CHUNKS + j)
        pbuf[:, j * FFN_CHUNK:(j + 1) * FFN_CHUNK] = (jax.nn.gelu(gate) * val).astype(BF16)
    f = jnp.dot(pbuf[...], wd_ref[...], preferred_element_type=F32)
    y = _ln(DEEPNORM_ALPHA * x_ref[...] + f, lg_ref[...], lb_ref[...])
    o_ref[...] = _to_natural(y) if last else y


def _ffn(x, p, chunks, last):
    rows_total = x.shape[0]
    rows = TILE_ROWS
    small = [p["w_up"], p["ffn_dw_w"], p["ffn_dw_b"], p["w_down"], p["ln2_g"], p["ln2_b"]]
    if last:
        out_spec = pl.BlockSpec((NV, TILE_STEPS, D_MODEL), lambda i: (0, i, 0))
        out_shape = jax.ShapeDtypeStruct((NV, rows_total // NV, D_MODEL), F32)
    else:
        out_spec = pl.BlockSpec((rows, D_MODEL), lambda i: (i, 0))
        out_shape = jax.ShapeDtypeStruct((rows_total, D_MODEL), F32)
    return pl.pallas_call(
        functools.partial(_ffn_body, rows=rows, chunks=chunks, last=last),
        grid=(rows_total // rows,),
        in_specs=_halo_specs(rows, FFN_HALO_ROWS, D_MODEL, rows_total)
        + [_const_spec(a.shape) for a in small],
        out_specs=out_spec,
        out_shape=out_shape,
        scratch_shapes=[
            pltpu.VMEM((rows + 2 * FFN_HALO_ROWS, D_MODEL), F32),
            pltpu.VMEM((rows, D_FF), BF16),
        ],
        compiler_params=_params(),
        name="ffn",
    )(x, x, x, *small)


def _block_diag_gate(w_a, w_x):
    heads_per_half = RNN_HEADS // 2
    half_w = D_RNN // 2

    def bd(w):
        eye = jnp.eye(heads_per_half, dtype=w.dtype)
        return jnp.einsum("hij,hg->higj", w, eye).reshape(half_w, half_w)

    out = []
    for d in range(2):
        halves = []
        for hf in range(2):
            sl = slice(hf * heads_per_half, (hf + 1) * heads_per_half)
            halves.append(jnp.concatenate([bd(w_a[d, sl]), bd(w_x[d, sl])], axis=-1))
        out.append(jnp.stack(halves))
    return jnp.stack(out).astype(BF16)


def _layer_params(l, w_in, conv_dw_w, conv_dw_b, conv_ln_g, conv_ln_b, rnn_conv_w, rnn_conv_b,
                  rg_w_a, rg_b_a, rg_w_x, rg_b_x, rg_lambda, w_out, ln1_g, ln1_b, w_up,
                  ffn_dw_w, ffn_dw_b, w_down, ln2_g, ln2_b):
    nc = N_FFN_CHUNKS
    row = lambda a: a.reshape(1, -1)
    return {
        "w_in": w_in[l].astype(BF16),
        "conv_w": jnp.repeat(conv_dw_w[l], NV, axis=0),
        "conv_b": jnp.broadcast_to(row(conv_dw_b[l]), (NV, D_CONV)),
        "conv_ln_g": row(conv_ln_g[l]), "conv_ln_b": row(conv_ln_b[l]),
        "rnn_conv_w": rnn_conv_w[l].reshape(2 * RNN_CONV_WIDTH, D_RNN),
        "rnn_conv_b": rnn_conv_b[l],
        "w_gate": _block_diag_gate(rg_w_a[l], rg_w_x[l]),
        "b_a": rg_b_a[l], "b_x": rg_b_x[l], "lam": rg_lambda[l],
        "w_out": w_out[l].astype(BF16), "ln1_g": row(ln1_g[l]), "ln1_b": row(ln1_b[l]),
        "w_up": w_up[l].astype(BF16).reshape(D_MODEL, 2 * nc, FFN_CHUNK).transpose(1, 0, 2),
        "ffn_dw_w": ffn_dw_w[l].reshape(3, 2 * nc, FFN_CHUNK).transpose(1, 0, 2),
        "ffn_dw_b": ffn_dw_b[l].reshape(2 * nc, 1, FFN_CHUNK),
        "w_down": w_down[l].astype(BF16),
        "ln2_g": row(ln2_g[l]), "ln2_b": row(ln2_b[l]),
    }


def _trunk(x, ln_in_g, ln_in_b, layers):
    bsz, seq, _ = x.shape
    chunks = NV // bsz
    sv = seq // chunks
    xt = x.reshape(NV, sv, D_MODEL)
    ln_g = ln_in_g.reshape(1, -1)
    ln_b = ln_in_b.reshape(1, -1)
    for l, p in enumerate(layers):
        outs = _proj_rnn(xt, ln_g, ln_b, p, chunks, first=(l == 0))
        if l == 0:
            xt, outs = outs[0], outs[1:]
        g, gr, s, pb, hb0, pb0 = outs[:6]
        cb, qb, eb = _carry(hb0, pb0)
        extra = (outs[6], qb, outs[7], eb) if chunks > 1 else None
        x1 = _mix_out(xt, g, gr, s, pb, cb, extra, p, chunks)
        xt = _ffn(x1, p, chunks, last=(l == len(layers) - 1))
    return xt.reshape(bsz, seq, D_MODEL)


def kernel(x_prompt, x_sample, ln_in_g, ln_in_b, w_in, conv_dw_w, conv_dw_b, conv_ln_g,
           conv_ln_b, rnn_conv_w, rnn_conv_b, rg_w_a, rg_b_a, rg_w_x, rg_b_x, rg_lambda,
           w_out, ln1_g, ln1_b, w_up, ffn_dw_w, ffn_dw_b, w_down, ln2_g, ln2_b):
    layers = [
        _layer_params(l, w_in, conv_dw_w, conv_dw_b, conv_ln_g, conv_ln_b, rnn_conv_w,
                      rnn_conv_b, rg_w_a, rg_b_a, rg_w_x, rg_b_x, rg_lambda, w_out, ln1_g,
                      ln1_b, w_up, ffn_dw_w, ffn_dw_b, w_down, ln2_g, ln2_b)
        for l in range(DEPTH)
    ]
    y_prompt = _trunk(x_prompt, ln_in_g, ln_in_b, layers)
    y_sample = _trunk(x_sample, ln_in_g, ln_in_b, layers)
    return (y_prompt, y_sample)
```

```python
import functools

import jax
import jax.numpy as jnp
from jax import lax
from jax.experimental import pallas as pl
from jax.experimental.pallas import tpu as pltpu

D_MODEL = 1024
D_CONV = 512
D_RNN = 512
D_IN = 2 * D_CONV + 2 * D_RNN
CONV_WIDTH = 31
RNN_HEADS = 8
RNN_CONV_WIDTH = 4
LRU_C = 8.0
D_FF = 2816
FFN_CHUNK = 256
N_FFN_CHUNKS = D_FF // FFN_CHUNK
LN_EPS = 1e-5
DEPTH = 2
DEEPNORM_ALPHA = (2.0 * DEPTH) ** 0.25

NV = 8
CONV_HALO_ROWS = 128
RNN_HALO_ROWS = 32
FFN_HALO_ROWS = NV
TILE_ROWS = 512
TILE_STEPS = TILE_ROWS // NV
NAT_HALO_STEPS = 8
CONV_ROW_CHUNK = 64
LANES = 128
VMEM_LIMIT = 56 * 1024 * 1024

F32 = jnp.float32
BF16 = jnp.bfloat16


def _ln(x, g, b):
    mu = jnp.mean(x, axis=-1, keepdims=True)
    xc = x - mu
    var = jnp.mean(xc * xc, axis=-1, keepdims=True)
    return xc * lax.rsqrt(var + LN_EPS) * g + b


def _sigmoid(x):
    return 0.5 * jnp.tanh(0.5 * x) + 0.5


GELU_C = (2.0 / 3.141592653589793) ** 0.5


def _gelu(x):
    inner = x * (GELU_C + (GELU_C * 0.044715) * (x * x))
    half_x = 0.5 * x
    return half_x + half_x * jnp.tanh(inner)


def _const_spec(shape):
    nd = len(shape)
    return pl.BlockSpec(shape, lambda *_: (0,) * nd, pipeline_mode=pl.Buffered(1))


def _params():
    return pltpu.CompilerParams(
        dimension_semantics=("arbitrary",), vmem_limit_bytes=VMEM_LIMIT)


def _to_time_major(x3):
    return pltpu.einshape("vtd->tvd", x3).reshape(x3.shape[1] * NV, x3.shape[2])


def _to_natural(x2):
    return pltpu.einshape("tvd->vtd", x2.reshape(x2.shape[0] // NV, NV, x2.shape[1]))


def _edge_halos(prev, nxt, chunks):
    halo = prev.shape[0]
    if chunks == 1:
        return jnp.zeros_like(prev), jnp.zeros_like(nxt)
    piece = lax.broadcasted_iota(jnp.int32, prev.shape, 0) & (chunks - 1)
    prev_edge = jnp.where(piece != 0, pltpu.roll(prev, 1, axis=0), 0.0)
    next_edge = jnp.where(piece != chunks - 1, pltpu.roll(nxt, halo - 1, axis=0), 0.0)
    return prev_edge, next_edge


def _fill_with_halo(buf, prev, main, nxt, chunks, is_first=None, is_last=None):
    halo = prev.shape[0]
    rows = main.shape[0]
    if is_first is None:
        is_first = pl.program_id(0) == 0
        is_last = pl.program_id(0) == pl.num_programs(0) - 1
    prev_edge, next_edge = _edge_halos(prev, nxt, chunks)
    buf[0:halo, :] = jnp.where(is_first, prev_edge, prev)
    buf[halo:halo + rows, :] = main
    buf[halo + rows:, :] = jnp.where(is_last, next_edge, nxt)


def _halo_specs(rows, halo, width, n_rows_total, tile=lambda i: i):
    per = rows // halo
    nblk = n_rows_total // halo
    return [
        pl.BlockSpec((halo, width), lambda i: ((tile(i) * per + nblk - 1) % nblk, 0)),
        pl.BlockSpec((rows, width), lambda i: (tile(i), 0)),
        pl.BlockSpec((halo, width), lambda i: (((tile(i) + 1) * per) % nblk, 0)),
    ]


def _natural_halo_specs(steps, width, n_steps_total, tile=lambda i: i):
    hs = NAT_HALO_STEPS
    per = steps // hs
    nblk = n_steps_total // hs
    return [
        pl.BlockSpec((NV, hs, width), lambda i: (0, (tile(i) * per + nblk - 1) % nblk, 0)),
        pl.BlockSpec((NV, steps, width), lambda i: (0, tile(i), 0)),
        pl.BlockSpec((NV, hs, width), lambda i: (0, ((tile(i) + 1) * per) % nblk, 0)),
    ]


def _proj_rnn_body(*refs, rows, chunks, first):
    (xp_ref, x_ref, xn_ref, lg_ref, lb_ref, w_ref,
     rw_ref, rb_ref, wg_ref, ba_ref, bx_ref, lam_ref) = refs[:12]
    outs = list(refs[12:])
    xo_ref = outs.pop(0) if first else None
    g_ref, gr_ref, s_ref, pb_ref, hb0_ref, pb0_ref = outs[:6]
    outs = outs[6:]
    if chunks > 1:
        qf_ref, ef_ref = outs[:2]
        outs = outs[2:]
    xbuf, rbuf, a_buf, b_buf, sf = outs[:5]
    sq = outs[5] if chunks > 1 else None

    halo = RNN_HALO_ROWS
    i = pl.program_id(0)

    if first:
        nat_halo = NAT_HALO_STEPS * NV
        xbuf[0:halo, :] = _to_time_major(xp_ref[...])[nat_halo - halo:]
        xbuf[halo:halo + rows, :] = _to_time_major(x_ref[...])
        xbuf[halo + rows:, :] = _to_time_major(xn_ref[...])[0:halo]
        x = _ln(xbuf[...], lg_ref[...], lb_ref[...])
        xo_ref[...] = x[halo:halo + rows]
    else:
        xbuf[0:halo, :] = xp_ref[...]
        xbuf[halo:halo + rows, :] = x_ref[...]
        xbuf[halo + rows:, :] = xn_ref[...]
        x = xbuf[...]
    xb16 = x.astype(BF16)
    xm16 = xb16[halo:halo + rows]
    rx = jnp.dot(xb16, w_ref[:, 2 * D_CONV:2 * D_CONV + D_RNN], preferred_element_type=F32)
    _fill_with_halo(rbuf, rx[0:halo], rx[halo:halo + rows], rx[halo + rows:], chunks)

    half_w = D_RNN // 2
    for d in range(2):
        if d == 0:
            hc = jnp.dot(xm16, w_ref[:, 0:2 * D_CONV], preferred_element_type=F32)
            g_ref[...] = hc[:, :D_CONV] * _sigmoid(hc[:, D_CONV:])
        else:
            hg = jnp.dot(xm16, w_ref[:, 2 * D_CONV + D_RNN:], preferred_element_type=F32)
            gr_ref[...] = _gelu(hg)
        xd = jnp.broadcast_to(rb_ref[d:d + 1, :], (rows, D_RNN))
        for k in range(RNN_CONV_WIDTH):
            shift = (k - (RNN_CONV_WIDTH - 1)) if d == 0 else k
            start = halo + shift * NV
            tap = d * RNN_CONV_WIDTH + k
            xd = xd + rbuf[start:start + rows, :] * rw_ref[tap:tap + 1, :]
        xb = xd.astype(BF16)
        parts = [jnp.dot(xb[:, hf * half_w:(hf + 1) * half_w], wg_ref[d, hf],
                         preferred_element_type=F32) for hf in range(2)]
        gate_a = jnp.concatenate([p[:, :half_w] for p in parts], axis=-1) + ba_ref[d:d + 1, :]
        gate_x = jnp.concatenate([p[:, half_w:] for p in parts], axis=-1) + bx_ref[d:d + 1, :]
        z = -lam_ref[d:d + 1, :]
        softplus = jnp.maximum(z, 0.0) + jnp.log1p(jnp.exp(-jnp.abs(z)))
        log_a = (-LRU_C) * _sigmoid(gate_a) * softplus
        a = jnp.exp(log_a)
        one_minus_a2 = -jnp.tanh(log_a) * (a * a + 1.0)
        root = jnp.where(one_minus_a2 > 0.0, one_minus_a2 * lax.rsqrt(one_minus_a2), 0.0)
        a_buf[d] = a
        b_buf[d] = root * (_sigmoid(gate_x) * xd)

    @pl.when(i == 0)
    def _():
        sf[...] = jnp.zeros_like(sf)
        if chunks > 1:
            sq[...] = jnp.ones_like(sq)

    nsteps = rows // NV

    def make_step(accumulate):
        def step(t, carry):
            hf, qf, hb, pb = carry
            rf = pl.multiple_of(t * NV, NV)
            rb = pl.multiple_of((nsteps - 1 - t) * NV, NV)
            af = a_buf[0, pl.ds(rf, NV), :]
            hf = af * hf + b_buf[0, pl.ds(rf, NV), :]
            ab = a_buf[1, pl.ds(rb, NV), :]
            hb = ab * hb + b_buf[1, pl.ds(rb, NV), :]
            pb = ab * pb
            pb_ref[pl.ds(rb, NV), :] = pb
            if chunks > 1:
                qf = af * qf
                qf_ref[pl.ds(rf, NV), :] = qf
            if accumulate:
                s_ref[pl.ds(rf, NV), :] += hf
                s_ref[pl.ds(rb, NV), :] += hb
            else:
                s_ref[pl.ds(rf, NV), :] = hf
                s_ref[pl.ds(rb, NV), :] = hb
            return hf, qf, hb, pb
        return step

    zeros = jnp.zeros((NV, D_RNN), F32)
    ones = jnp.ones((NV, D_RNN), F32)
    q0 = sq[...] if chunks > 1 else ones
    carry = (sf[...], q0, zeros, ones)
    carry = lax.fori_loop(0, nsteps // 2, make_step(False), carry, unroll=8)
    hf, qf, hb, pb = lax.fori_loop(nsteps // 2, nsteps, make_step(True), carry, unroll=8)
    sf[...] = hf
    hb0_ref[...] = hb
    pb0_ref[...] = pb
    if chunks > 1:
        sq[...] = qf
        ef_ref[...] = hf


def _proj_rnn(x, ln_g, ln_b, p, chunks, first):
    rows = TILE_ROWS
    halo = RNN_HALO_ROWS
    if first:
        rows_total = x.shape[1] * NV
        x_specs = _natural_halo_specs(TILE_STEPS, D_MODEL, x.shape[1])
    else:
        rows_total = x.shape[0]
        x_specs = _halo_specs(rows, halo, D_MODEL, rows_total)
    nt = rows_total // rows
    small = [ln_g, ln_b, p["w_in"], p["rnn_conv_w"], p["rnn_conv_b"], p["w_gate"],
             p["b_a"], p["b_x"], p["lam"]]
    blk = (NV, D_RNN)
    wide = pl.BlockSpec((rows, D_MODEL), lambda i: (i, 0))
    narrow = pl.BlockSpec((rows, D_RNN), lambda i: (i, 0))
    per_tile = pl.BlockSpec(blk, lambda i: (i, 0))
    big = jax.ShapeDtypeStruct((rows_total, D_RNN), F32)
    summary = jax.ShapeDtypeStruct((nt * NV, D_RNN), F32)
    out_specs = ([wide] if first else []) + [narrow] * 4 + [per_tile] * 2
    out_shape = ([jax.ShapeDtypeStruct((rows_total, D_MODEL), F32)] if first else []) \
        + [big] * 4 + [summary] * 2
    scratch = [
        pltpu.VMEM((rows + 2 * halo, D_MODEL), F32),
        pltpu.VMEM((rows + 2 * halo, D_RNN), F32),
        pltpu.VMEM((2, rows, D_RNN), F32),
        pltpu.VMEM((2, rows, D_RNN), F32),
        pltpu.VMEM(blk, F32),
    ]
    if chunks > 1:
        out_specs += [narrow, pl.BlockSpec(blk, lambda i: (0, 0))]
        out_shape += [big, jax.ShapeDtypeStruct(blk, F32)]
        scratch.append(pltpu.VMEM(blk, F32))
    return pl.pallas_call(
        functools.partial(_proj_rnn_body, rows=rows, chunks=chunks, first=first),
        grid=(nt,),
        in_specs=x_specs + [_const_spec(a.shape) for a in small],
        out_specs=out_specs,
        out_shape=out_shape,
        scratch_shapes=scratch,
        compiler_params=_params(),
        name="proj_rnn",
    )(x, x, x, *small)


def _carry_body(hb0_ref, pb0_ref, cb_ref, qb_ref, eb_ref, *, nt):
    def step(k, carry):
        sb, ub = carry
        r = pl.multiple_of((nt - 1 - k) * NV, NV)
        cb_ref[pl.ds(r, NV), :] = sb
        qb_ref[pl.ds(r, NV), :] = ub
        p0 = pb0_ref[pl.ds(r, NV), :]
        return hb0_ref[pl.ds(r, NV), :] + p0 * sb, p0 * ub

    init = (jnp.zeros((NV, D_RNN), F32), jnp.ones((NV, D_RNN), F32))
    sb, _ = lax.fori_loop(0, nt, step, init)
    eb_ref[...] = sb


def _carry(hb0, pb0):
    nt = hb0.shape[0] // NV
    return pl.pallas_call(
        functools.partial(_carry_body, nt=nt),
        out_shape=[jax.ShapeDtypeStruct((nt * NV, D_RNN), F32)] * 2
        + [jax.ShapeDtypeStruct((NV, D_RNN), F32)],
        compiler_params=pltpu.CompilerParams(vmem_limit_bytes=VMEM_LIMIT),
        name="carry",
    )(hb0, pb0)


def _mix_out_body(*refs, rows, chunks):
    (gp_ref, g_ref, gn_ref, x_ref, s_ref, pb_ref, gr_ref, cb_ref) = refs[:8]
    refs = refs[8:]
    if chunks > 1:
        qf_ref, qb_ref, ef_ref, eb_ref = refs[:4]
        refs = refs[4:]
    (cw_ref, cb_bias_ref, clg_ref, clb_ref, wo_ref, lg_ref, lb_ref,
     o_ref, gbuf, cbuf) = refs

    halo = CONV_HALO_ROWS
    i = pl.program_id(0)
    gp = gp_ref[...]
    gn = gn_ref[...]
    gp_edge, gn_edge = _edge_halos(gp, gn, chunks)
    gp = jnp.where(i == 0, gp_edge, gp)
    gn = jnp.where(i == pl.num_programs(0) - 1, gn_edge, gn)
    for col in range(D_CONV // LANES):
        lanes = slice(col * LANES, (col + 1) * LANES)
        gbuf[col, 0:halo, :] = gp[:, lanes]
        gbuf[col, halo:halo + rows, :] = g_ref[:, lanes]
        gbuf[col, halo + rows:, :] = gn[:, lanes]

    half = CONV_WIDTH // 2

    conv_shape = (CONV_ROW_CHUNK // NV, NV, LANES)
    for col in range(D_CONV // LANES):
        lanes = slice(col * LANES, (col + 1) * LANES)
        taps = [cw_ref[k * NV:(k + 1) * NV, lanes][None] for k in range(CONV_WIDTH)]
        bias = jnp.broadcast_to(cb_bias_ref[:, lanes][None], conv_shape)

        def conv_chunk(j, _, col=col, lanes=lanes, taps=taps, bias=bias):
            base = pl.multiple_of(j * CONV_ROW_CHUNK, CONV_ROW_CHUNK)
            acc = [bias, None]
            for k in range(CONV_WIDTH):
                start = base + (halo + (k - half) * NV)
                term = gbuf[col, pl.ds(start, CONV_ROW_CHUNK), :].reshape(conv_shape) * taps[k]
                acc[k % 2] = term if acc[k % 2] is None else acc[k % 2] + term
            total = acc[0] + acc[1]
            cbuf[pl.ds(base, CONV_ROW_CHUNK), lanes] = total.reshape(CONV_ROW_CHUNK, LANES)
            return 0

        lax.fori_loop(0, rows // CONV_ROW_CHUNK, conv_chunk, 0)
    cn = _ln(cbuf[...], clg_ref[...], clb_ref[...])
    c = cn * _sigmoid(cn)

    nsteps = rows // NV
    shape3 = (nsteps, NV, D_RNN)
    cb = cb_ref[...]
    rec = s_ref[...].reshape(shape3)
    if chunks > 1:
        ef, eb = _edge_halos(ef_ref[...], eb_ref[...], chunks)
        cb = cb + qb_ref[...] * eb
        rec = rec + qf_ref[...].reshape(shape3) * ef[None]
    rec = rec + pb_ref[...].reshape(shape3) * cb[None]
    rec = rec.reshape(rows, D_RNN) * gr_ref[...]

    mix = (jnp.dot(c.astype(BF16), wo_ref[0:D_CONV, :], preferred_element_type=F32)
           + jnp.dot(rec.astype(BF16), wo_ref[D_CONV:, :], preferred_element_type=F32))
    o_ref[...] = _ln(DEEPNORM_ALPHA * x_ref[...] + mix, lg_ref[...], lb_ref[...])


def _mix_out(x, g, gr, s, pb, cb, extra, p, chunks):
    rows_total = x.shape[0]
    rows = TILE_ROWS
    halo = CONV_HALO_ROWS
    blk = (NV, D_RNN)
    wide = pl.BlockSpec((rows, D_MODEL), lambda i: (i, 0))
    narrow = pl.BlockSpec((rows, D_RNN), lambda i: (i, 0))
    per_tile = pl.BlockSpec(blk, lambda i: (i, 0))
    small = [p["conv_w"], p["conv_b"], p["conv_ln_g"], p["conv_ln_b"],
             p["w_out"], p["ln1_g"], p["ln1_b"]]
    in_specs = _halo_specs(rows, halo, D_CONV, rows_total) + [wide] + [narrow] * 3 + [per_tile]
    args = [g, g, g, x, s, pb, gr, cb]
    if chunks > 1:
        qf, qb, ef, eb = extra
        in_specs += [narrow, per_tile, _const_spec(blk), _const_spec(blk)]
        args += [qf, qb, ef, eb]
    return pl.pallas_call(
        functools.partial(_mix_out_body, rows=rows, chunks=chunks),
        grid=(rows_total // rows,),
        in_specs=in_specs + [_const_spec(a.shape) for a in small],
        out_specs=wide,
        out_shape=jax.ShapeDtypeStruct((rows_total, D_MODEL), F32),
        scratch_shapes=[
            pltpu.VMEM((D_CONV // LANES, rows + 2 * halo, LANES), F32),
            pltpu.VMEM((rows, D_CONV), F32),
        ],
        compiler_params=_params(),
        name="mix_out",
    )(*args, *small)


def _ffn_body(xp_ref, x_ref, xn_ref, wu_ref, dw_ref, db_ref, wd_ref, lg_ref, lb_ref,
              o_ref, xbuf, pbuf, *, rows, chunks, last):
    _fill_with_halo(xbuf, xp_ref[...], x_ref[...], xn_ref[...], chunks)
    xe = xbuf[...].astype(BF16)

    def conv3(u, idx):
        w = dw_ref[idx]
        return (u[0:rows] * w[0:1, :] + u[NV:NV + rows] * w[1:2, :]
                + u[2 * NV:2 * NV + rows] * w[2:3, :] + db_ref[idx])

    for j in range(N_FFN_CHUNKS):
        uv = jnp.dot(xe, wu_ref[j], preferred_element_type=F32)
        ug = jnp.dot(xe, wu_ref[N_FFN_CHUNKS + j], preferred_element_type=F32)
        val = conv3(uv, j)
        gate = conv3(ug, N_FFN_CHUNKS + j)
        pbuf[:, j * FFN_CHUNK:(j + 1) * FFN_CHUNK] = (_gelu(gate) * val).astype(BF16)
    f = jnp.dot(pbuf[...], wd_ref[...], preferred_element_type=F32)
    y = _ln(DEEPNORM_ALPHA * x_ref[...] + f, lg_ref[...], lb_ref[...])
    o_ref[...] = _to_natural(y) if last else y


def _ffn(x, p, chunks, last):
    rows_total = x.shape[0]
    rows = TILE_ROWS
    small = [p["w_up"], p["ffn_dw_w"], p["ffn_dw_b"], p["w_down"], p["ln2_g"], p["ln2_b"]]
    if last:
        out_spec = pl.BlockSpec((NV, TILE_STEPS, D_MODEL), lambda i: (0, i, 0))
        out_shape = jax.ShapeDtypeStruct((NV, rows_total // NV, D_MODEL), F32)
    else:
        out_spec = pl.BlockSpec((rows, D_MODEL), lambda i: (i, 0))
        out_shape = jax.ShapeDtypeStruct((rows_total, D_MODEL), F32)
    return pl.pallas_call(
        functools.partial(_ffn_body, rows=rows, chunks=chunks, last=last),
        grid=(rows_total // rows,),
        in_specs=_halo_specs(rows, FFN_HALO_ROWS, D_MODEL, rows_total)
        + [_const_spec(a.shape) for a in small],
        out_specs=out_spec,
        out_shape=out_shape,
        scratch_shapes=[
            pltpu.VMEM((rows + 2 * FFN_HALO_ROWS, D_MODEL), F32),
            pltpu.VMEM((rows, D_FF), BF16),
        ],
        compiler_params=_params(),
        name="ffn",
    )(x, x, x, *small)


def _block_diag_gate(w_a, w_x):
    heads_per_half = RNN_HEADS // 2
    half_w = D_RNN // 2

    def bd(w):
        eye = jnp.eye(heads_per_half, dtype=w.dtype)
        return jnp.einsum("hij,hg->higj", w, eye).reshape(half_w, half_w)

    out = []
    for d in range(2):
        halves = []
        for hf in range(2):
            sl = slice(hf * heads_per_half, (hf + 1) * heads_per_half)
            halves.append(jnp.concatenate([bd(w_a[d, sl]), bd(w_x[d, sl])], axis=-1))
        out.append(jnp.stack(halves))
    return jnp.stack(out).astype(BF16)


def _layer_params(l, w_in, conv_dw_w, conv_dw_b, conv_ln_g, conv_ln_b, rnn_conv_w, rnn_conv_b,
                  rg_w_a, rg_b_a, rg_w_x, rg_b_x, rg_lambda, w_out, ln1_g, ln1_b, w_up,
                  ffn_dw_w, ffn_dw_b, w_down, ln2_g, ln2_b):
    nc = N_FFN_CHUNKS
    row = lambda a: a.reshape(1, -1)
    return {
        "w_in": w_in[l].astype(BF16),
        "conv_w": jnp.repeat(conv_dw_w[l], NV, axis=0),
        "conv_b": jnp.broadcast_to(row(conv_dw_b[l]), (NV, D_CONV)),
        "conv_ln_g": row(conv_ln_g[l]), "conv_ln_b": row(conv_ln_b[l]),
        "rnn_conv_w": rnn_conv_w[l].reshape(2 * RNN_CONV_WIDTH, D_RNN),
        "rnn_conv_b": rnn_conv_b[l],
        "w_gate": _block_diag_gate(rg_w_a[l], rg_w_x[l]),
        "b_a": rg_b_a[l], "b_x": rg_b_x[l], "lam": rg_lambda[l],
        "w_out": w_out[l].astype(BF16), "ln1_g": row(ln1_g[l]), "ln1_b": row(ln1_b[l]),
        "w_up": w_up[l].astype(BF16).reshape(D_MODEL, 2 * nc, FFN_CHUNK).transpose(1, 0, 2),
        "ffn_dw_w": ffn_dw_w[l].reshape(3, 2 * nc, FFN_CHUNK).transpose(1, 0, 2),
        "ffn_dw_b": ffn_dw_b[l].reshape(2 * nc, 1, FFN_CHUNK),
        "w_down": w_down[l].astype(BF16),
        "ln2_g": row(ln2_g[l]), "ln2_b": row(ln2_b[l]),
    }


def _trunk(x, ln_in_g, ln_in_b, layers):
    bsz, seq, _ = x.shape
    chunks = NV // bsz
    sv = seq // chunks
    xt = x.reshape(NV, sv, D_MODEL)
    ln_g = ln_in_g.reshape(1, -1)
    ln_b = ln_in_b.reshape(1, -1)
    for l, p in enumerate(layers):
        outs = _proj_rnn(xt, ln_g, ln_b, p, chunks, first=(l == 0))
        if l == 0:
            xt, outs = outs[0], outs[1:]
        g, gr, s, pb, hb0, pb0 = outs[:6]
        cb, qb, eb = _carry(hb0, pb0)
        extra = (outs[6], qb, outs[7], eb) if chunks > 1 else None
        x1 = _mix_out(xt, g, gr, s, pb, cb, extra, p, chunks)
        xt = _ffn(x1, p, chunks, last=(l == len(layers) - 1))
    return xt.reshape(bsz, seq, D_MODEL)


def kernel(x_prompt, x_sample, ln_in_g, ln_in_b, w_in, conv_dw_w, conv_dw_b, conv_ln_g,
           conv_ln_b, rnn_conv_w, rnn_conv_b, rg_w_a, rg_b_a, rg_w_x, rg_b_x, rg_lambda,
           w_out, ln1_g, ln1_b, w_up, ffn_dw_w, ffn_dw_b, w_down, ln2_g, ln2_b):
    layers = [
        _layer_params(l, w_in, conv_dw_w, conv_dw_b, conv_ln_g, conv_ln_b, rnn_conv_w,
                      rnn_conv_b, rg_w_a, rg_b_a, rg_w_x, rg_b_x, rg_lambda, w_out, ln1_g,
                      ln1_b, w_up, ffn_dw_w, ffn_dw_b, w_down, ln2_g, ln2_b)
        for l in range(DEPTH)
    ]
    y_prompt = _trunk(x_prompt, ln_in_g, ln_in_b, layers)
    y_sample = _trunk(x_sample, ln_in_g, ln_in_b, layers)
    return (y_prompt, y_sample)
```

```python
import functools

import jax
import jax.numpy as jnp
from jax import lax
from jax.experimental import pallas as pl
from jax.experimental.pallas import tpu as pltpu

D_MODEL = 1024
D_CONV = 512
D_RNN = 512
D_IN = 2 * D_CONV + 2 * D_RNN
CONV_WIDTH = 31
RNN_HEADS = 8
RNN_CONV_WIDTH = 4
LRU_C = 8.0
D_FF = 2816
FFN_CHUNK = 256
N_FFN_CHUNKS = D_FF // FFN_CHUNK
LN_EPS = 1e-5
DEPTH = 2
DEEPNORM_ALPHA = (2.0 * DEPTH) ** 0.25

NV = 8
CONV_HALO_ROWS = 128
RNN_HALO_ROWS = 32
FFN_HALO_ROWS = NV
TILE_ROWS = 512
TILE_STEPS = TILE_ROWS // NV
FFN_TILE_ROWS = 1024
FFN_OUT_ROWS = 256
NAT_HALO_STEPS = 8
CONV_ROW_CHUNK = 64
LANES = 128
VMEM_LIMIT = 56 * 1024 * 1024

F32 = jnp.float32
BF16 = jnp.bfloat16


def _ln(x, g, b):
    mu = jnp.mean(x, axis=-1, keepdims=True)
    xc = x - mu
    var = jnp.mean(xc * xc, axis=-1, keepdims=True)
    return xc * lax.rsqrt(var + LN_EPS) * g + b


def _sigmoid(x):
    return 0.5 * jnp.tanh(0.5 * x) + 0.5


GELU_C = (2.0 / 3.141592653589793) ** 0.5


def _gelu(x):
    inner = x * (GELU_C + (GELU_C * 0.044715) * (x * x))
    half_x = 0.5 * x
    return half_x + half_x * jnp.tanh(inner)


def _const_spec(shape):
    nd = len(shape)
    return pl.BlockSpec(shape, lambda *_: (0,) * nd, pipeline_mode=pl.Buffered(1))


def _params():
    return pltpu.CompilerParams(
        dimension_semantics=("arbitrary",), vmem_limit_bytes=VMEM_LIMIT)


def _to_time_major(x3):
    return pltpu.einshape("vtd->tvd", x3).reshape(x3.shape[1] * NV, x3.shape[2])


def _to_natural(x2):
    return pltpu.einshape("tvd->vtd", x2.reshape(x2.shape[0] // NV, NV, x2.shape[1]))


def _edge_halos(prev, nxt, chunks):
    halo = prev.shape[0]
    if chunks == 1:
        return jnp.zeros_like(prev), jnp.zeros_like(nxt)
    piece = lax.broadcasted_iota(jnp.int32, prev.shape, 0) & (chunks - 1)
    prev_edge = jnp.where(piece != 0, pltpu.roll(prev, 1, axis=0), 0.0)
    next_edge = jnp.where(piece != chunks - 1, pltpu.roll(nxt, halo - 1, axis=0), 0.0)
    return prev_edge, next_edge


def _fill_with_halo(buf, prev, main, nxt, chunks, is_first=None, is_last=None):
    halo = prev.shape[0]
    rows = main.shape[0]
    if is_first is None:
        is_first = pl.program_id(0) == 0
        is_last = pl.program_id(0) == pl.num_programs(0) - 1
    prev_edge, next_edge = _edge_halos(prev, nxt, chunks)
    buf[0:halo, :] = jnp.where(is_first, prev_edge, prev)
    buf[halo:halo + rows, :] = main
    buf[halo + rows:, :] = jnp.where(is_last, next_edge, nxt)


def _halo_specs(rows, halo, width, n_rows_total, tile=lambda i: i):
    per = rows // halo
    nblk = n_rows_total // halo
    return [
        pl.BlockSpec((halo, width), lambda i: ((tile(i) * per + nblk - 1) % nblk, 0)),
        pl.BlockSpec((rows, width), lambda i: (tile(i), 0)),
        pl.BlockSpec((halo, width), lambda i: (((tile(i) + 1) * per) % nblk, 0)),
    ]


def _natural_halo_specs(steps, width, n_steps_total, tile=lambda i: i):
    hs = NAT_HALO_STEPS
    per = steps // hs
    nblk = n_steps_total // hs
    return [
        pl.BlockSpec((NV, hs, width), lambda i: (0, (tile(i) * per + nblk - 1) % nblk, 0)),
        pl.BlockSpec((NV, steps, width), lambda i: (0, tile(i), 0)),
        pl.BlockSpec((NV, hs, width), lambda i: (0, ((tile(i) + 1) * per) % nblk, 0)),
    ]


def _proj_rnn_body(*refs, rows, chunks, first):
    (xp_ref, x_ref, xn_ref, lg_ref, lb_ref, w_ref,
     rw_ref, rb_ref, wg_ref, ba_ref, bx_ref, lam_ref) = refs[:12]
    outs = list(refs[12:])
    xo_ref = outs.pop(0) if first else None
    g_ref, gr_ref, s_ref, pb_ref, hb0_ref, pb0_ref = outs[:6]
    outs = outs[6:]
    if chunks > 1:
        qf_ref, ef_ref = outs[:2]
        outs = outs[2:]
    xbuf, rbuf, a_buf, b_buf, sf = outs[:5]
    sq = outs[5] if chunks > 1 else None

    halo = RNN_HALO_ROWS
    i = pl.program_id(0)

    if first:
        nat_halo = NAT_HALO_STEPS * NV
        xbuf[0:halo, :] = _to_time_major(xp_ref[...])[nat_halo - halo:]
        xbuf[halo:halo + rows, :] = _to_time_major(x_ref[...])
        xbuf[halo + rows:, :] = _to_time_major(xn_ref[...])[0:halo]
        x = _ln(xbuf[...], lg_ref[...], lb_ref[...])
        xo_ref[...] = x[halo:halo + rows]
    else:
        xbuf[0:halo, :] = xp_ref[...]
        xbuf[halo:halo + rows, :] = x_ref[...]
        xbuf[halo + rows:, :] = xn_ref[...]
        x = xbuf[...]
    xb16 = x.astype(BF16)
    xm16 = xb16[halo:halo + rows]
    rx = jnp.dot(xb16, w_ref[:, 2 * D_CONV:2 * D_CONV + D_RNN], preferred_element_type=F32)
    _fill_with_halo(rbuf, rx[0:halo], rx[halo:halo + rows], rx[halo + rows:], chunks)

    half_w = D_RNN // 2
    for d in range(2):
        if d == 0:
            hc = jnp.dot(xm16, w_ref[:, 0:2 * D_CONV], preferred_element_type=F32)
            g_ref[...] = hc[:, :D_CONV] * _sigmoid(hc[:, D_CONV:])
        else:
            hg = jnp.dot(xm16, w_ref[:, 2 * D_CONV + D_RNN:], preferred_element_type=F32)
            gr_ref[...] = _gelu(hg)
        xd = jnp.broadcast_to(rb_ref[d:d + 1, :], (rows, D_RNN))
        for k in range(RNN_CONV_WIDTH):
            shift = (k - (RNN_CONV_WIDTH - 1)) if d == 0 else k
            start = halo + shift * NV
            tap = d * RNN_CONV_WIDTH + k
            xd = xd + rbuf[start:start + rows, :] * rw_ref[tap:tap + 1, :]
        xb = xd.astype(BF16)
        parts = [jnp.dot(xb[:, hf * half_w:(hf + 1) * half_w], wg_ref[d, hf],
                         preferred_element_type=F32) for hf in range(2)]
        gate_a = jnp.concatenate([p[:, :half_w] for p in parts], axis=-1) + ba_ref[d:d + 1, :]
        gate_x = jnp.concatenate([p[:, half_w:] for p in parts], axis=-1) + bx_ref[d:d + 1, :]
        z = -lam_ref[d:d + 1, :]
        softplus = jnp.maximum(z, 0.0) + jnp.log1p(jnp.exp(-jnp.abs(z)))
        log_a = (-LRU_C) * _sigmoid(gate_a) * softplus
        a = jnp.exp(log_a)
        one_minus_a2 = -jnp.tanh(log_a) * (a * a + 1.0)
        root = jnp.where(one_minus_a2 > 0.0, one_minus_a2 * lax.rsqrt(one_minus_a2), 0.0)
        a_buf[d] = a
        b_buf[d] = root * (_sigmoid(gate_x) * xd)

    @pl.when(i == 0)
    def _():
        sf[...] = jnp.zeros_like(sf)
        if chunks > 1:
            sq[...] = jnp.ones_like(sq)

    nsteps = rows // NV

    def make_step(accumulate):
        def step(t, carry):
            hf, qf, hb, pb = carry
            rf = pl.multiple_of(t * NV, NV)
            rb = pl.multiple_of((nsteps - 1 - t) * NV, NV)
            af = a_buf[0, pl.ds(rf, NV), :]
            hf = af * hf + b_buf[0, pl.ds(rf, NV), :]
            ab = a_buf[1, pl.ds(rb, NV), :]
            hb = ab * hb + b_buf[1, pl.ds(rb, NV), :]
            pb = ab * pb
            pb_ref[pl.ds(rb, NV), :] = pb
            if chunks > 1:
                qf = af * qf
                qf_ref[pl.ds(rf, NV), :] = qf
            if accumulate:
                s_ref[pl.ds(rf, NV), :] += hf
                s_ref[pl.ds(rb, NV), :] += hb
            else:
                s_ref[pl.ds(rf, NV), :] = hf
                s_ref[pl.ds(rb, NV), :] = hb
            return hf, qf, hb, pb
        return step

    zeros = jnp.zeros((NV, D_RNN), F32)
    ones = jnp.ones((NV, D_RNN), F32)
    q0 = sq[...] if chunks > 1 else ones
    carry = (sf[...], q0, zeros, ones)
    carry = lax.fori_loop(0, nsteps // 2, make_step(False), carry, unroll=8)
    hf, qf, hb, pb = lax.fori_loop(nsteps // 2, nsteps, make_step(True), carry, unroll=8)
    sf[...] = hf
    hb0_ref[...] = hb
    pb0_ref[...] = pb
    if chunks > 1:
        sq[...] = qf
        ef_ref[...] = hf


def _proj_rnn(x, ln_g, ln_b, p, chunks, first):
    rows = TILE_ROWS
    halo = RNN_HALO_ROWS
    if first:
        rows_total = x.shape[1] * NV
        x_specs = _natural_halo_specs(TILE_STEPS, D_MODEL, x.shape[1])
    else:
        rows_total = x.shape[0]
        x_specs = _halo_specs(rows, halo, D_MODEL, rows_total)
    nt = rows_total // rows
    small = [ln_g, ln_b, p["w_in"], p["rnn_conv_w"], p["rnn_conv_b"], p["w_gate"],
             p["b_a"], p["b_x"], p["lam"]]
    blk = (NV, D_RNN)
    wide = pl.BlockSpec((rows, D_MODEL), lambda i: (i, 0))
    narrow = pl.BlockSpec((rows, D_RNN), lambda i: (i, 0))
    per_tile = pl.BlockSpec(blk, lambda i: (i, 0))
    big = jax.ShapeDtypeStruct((rows_total, D_RNN), F32)
    summary = jax.ShapeDtypeStruct((nt * NV, D_RNN), F32)
    out_specs = ([wide] if first else []) + [narrow] * 4 + [per_tile] * 2
    out_shape = ([jax.ShapeDtypeStruct((rows_total, D_MODEL), F32)] if first else []) \
        + [big] * 4 + [summary] * 2
    scratch = [
        pltpu.VMEM((rows + 2 * halo, D_MODEL), F32),
        pltpu.VMEM((rows + 2 * halo, D_RNN), F32),
        pltpu.VMEM((2, rows, D_RNN), F32),
        pltpu.VMEM((2, rows, D_RNN), F32),
        pltpu.VMEM(blk, F32),
    ]
    if chunks > 1:
        out_specs += [narrow, pl.BlockSpec(blk, lambda i: (0, 0))]
        out_shape += [big, jax.ShapeDtypeStruct(blk, F32)]
        scratch.append(pltpu.VMEM(blk, F32))
    return pl.pallas_call(
        functools.partial(_proj_rnn_body, rows=rows, chunks=chunks, first=first),
        grid=(nt,),
        in_specs=x_specs + [_const_spec(a.shape) for a in small],
        out_specs=out_specs,
        out_shape=out_shape,
        scratch_shapes=scratch,
        compiler_params=_params(),
        name="proj_rnn",
    )(x, x, x, *small)


def _carry_body(hb0_ref, pb0_ref, cb_ref, qb_ref, eb_ref, *, nt):
    def step(k, carry):
        sb, ub = carry
        r = pl.multiple_of((nt - 1 - k) * NV, NV)
        cb_ref[pl.ds(r, NV), :] = sb
        qb_ref[pl.ds(r, NV), :] = ub
        p0 = pb0_ref[pl.ds(r, NV), :]
        return hb0_ref[pl.ds(r, NV), :] + p0 * sb, p0 * ub

    init = (jnp.zeros((NV, D_RNN), F32), jnp.ones((NV, D_RNN), F32))
    sb, _ = lax.fori_loop(0, nt, step, init)
    eb_ref[...] = sb


def _carry(hb0, pb0):
    nt = hb0.shape[0] // NV
    return pl.pallas_call(
        functools.partial(_carry_body, nt=nt),
        out_shape=[jax.ShapeDtypeStruct((nt * NV, D_RNN), F32)] * 2
        + [jax.ShapeDtypeStruct((NV, D_RNN), F32)],
        compiler_params=pltpu.CompilerParams(vmem_limit_bytes=VMEM_LIMIT),
        name="carry",
    )(hb0, pb0)


def _mix_out_body(*refs, rows, chunks):
    (gp_ref, g_ref, gn_ref, x_ref, s_ref, pb_ref, gr_ref, cb_ref) = refs[:8]
    refs = refs[8:]
    if chunks > 1:
        qf_ref, qb_ref, ef_ref, eb_ref = refs[:4]
        refs = refs[4:]
    (cw_ref, cb_bias_ref, clg_ref, clb_ref, wo_ref, o_ref, gbuf, cbuf) = refs

    halo = CONV_HALO_ROWS
    i = pl.program_id(0)
    gp = gp_ref[...]
    gn = gn_ref[...]
    gp_edge, gn_edge = _edge_halos(gp, gn, chunks)
    gp = jnp.where(i == 0, gp_edge, gp)
    gn = jnp.where(i == pl.num_programs(0) - 1, gn_edge, gn)
    for col in range(D_CONV // LANES):
        lanes = slice(col * LANES, (col + 1) * LANES)
        gbuf[col, 0:halo, :] = gp[:, lanes]
        gbuf[col, halo:halo + rows, :] = g_ref[:, lanes]
        gbuf[col, halo + rows:, :] = gn[:, lanes]

    half = CONV_WIDTH // 2

    conv_shape = (CONV_ROW_CHUNK // NV, NV, LANES)
    for col in range(D_CONV // LANES):
        lanes = slice(col * LANES, (col + 1) * LANES)
        taps = [cw_ref[k * NV:(k + 1) * NV, lanes][None] for k in range(CONV_WIDTH)]
        bias = jnp.broadcast_to(cb_bias_ref[:, lanes][None], conv_shape)

        def conv_chunk(j, _, col=col, lanes=lanes, taps=taps, bias=bias):
            base = pl.multiple_of(j * CONV_ROW_CHUNK, CONV_ROW_CHUNK)
            acc = [bias, None]
            for k in range(CONV_WIDTH):
                start = base + (halo + (k - half) * NV)
                term = gbuf[col, pl.ds(start, CONV_ROW_CHUNK), :].reshape(conv_shape) * taps[k]
                acc[k % 2] = term if acc[k % 2] is None else acc[k % 2] + term
            total = acc[0] + acc[1]
            cbuf[pl.ds(base, CONV_ROW_CHUNK), lanes] = total.reshape(CONV_ROW_CHUNK, LANES)
            return 0

        lax.fori_loop(0, rows // CONV_ROW_CHUNK, conv_chunk, 0)
    cn = _ln(cbuf[...], clg_ref[...], clb_ref[...])
    c = cn * _sigmoid(cn)

    nsteps = rows // NV
    shape3 = (nsteps, NV, D_RNN)
    cb = cb_ref[...]
    rec = s_ref[...].reshape(shape3)
    if chunks > 1:
        ef, eb = _edge_halos(ef_ref[...], eb_ref[...], chunks)
        cb = cb + qb_ref[...] * eb
        rec = rec + qf_ref[...].reshape(shape3) * ef[None]
    rec = rec + pb_ref[...].reshape(shape3) * cb[None]
    rec = rec.reshape(rows, D_RNN) * gr_ref[...]

    mix = (jnp.dot(c.astype(BF16), wo_ref[0:D_CONV, :], preferred_element_type=F32)
           + jnp.dot(rec.astype(BF16), wo_ref[D_CONV:, :], preferred_element_type=F32))
    o_ref[...] = DEEPNORM_ALPHA * x_ref[...] + mix


def _mix_out(x, g, gr, s, pb, cb, extra, p, chunks):
    rows_total = x.shape[0]
    rows = TILE_ROWS
    halo = CONV_HALO_ROWS
    blk = (NV, D_RNN)
    wide = pl.BlockSpec((rows, D_MODEL), lambda i: (i, 0))
    narrow = pl.BlockSpec((rows, D_RNN), lambda i: (i, 0))
    per_tile = pl.BlockSpec(blk, lambda i: (i, 0))
    small = [p["conv_w"], p["conv_b"], p["conv_ln_g"], p["conv_ln_b"], p["w_out"]]
    in_specs = _halo_specs(rows, halo, D_CONV, rows_total) + [wide] + [narrow] * 3 + [per_tile]
    args = [g, g, g, x, s, pb, gr, cb]
    if chunks > 1:
        qf, qb, ef, eb = extra
        in_specs += [narrow, per_tile, _const_spec(blk), _const_spec(blk)]
        args += [qf, qb, ef, eb]
    return pl.pallas_call(
        functools.partial(_mix_out_body, rows=rows, chunks=chunks),
        grid=(rows_total // rows,),
        in_specs=in_specs + [_const_spec(a.shape) for a in small],
        out_specs=wide,
        out_shape=jax.ShapeDtypeStruct((rows_total, D_MODEL), F32),
        scratch_shapes=[
            pltpu.VMEM((D_CONV // LANES, rows + 2 * halo, LANES), F32),
            pltpu.VMEM((rows, D_CONV), F32),
        ],
        compiler_params=_params(),
        name="mix_out",
    )(*args, *small)


def _ffn_body(xp_ref, x_ref, xn_ref, l1g_ref, l1b_ref, wu_ref, dw_ref, db_ref, wd_ref,
              lg_ref, lb_ref, o_ref, xbuf, pbuf, *, rows, chunks, last):
    norm1 = lambda ref: _ln(ref[...], l1g_ref[...], l1b_ref[...])
    _fill_with_halo(xbuf, norm1(xp_ref), norm1(x_ref), norm1(xn_ref), chunks)
    fh = FFN_HALO_ROWS
    xe = xbuf[...].astype(BF16)

    def conv3(u, idx):
        w = dw_ref[idx]
        return (u[0:rows] * w[0:1, :] + u[NV:NV + rows] * w[1:2, :]
                + u[2 * NV:2 * NV + rows] * w[2:3, :] + db_ref[idx])

    for j in range(N_FFN_CHUNKS):
        uv = jnp.dot(xe, wu_ref[j], preferred_element_type=F32)
        ug = jnp.dot(xe, wu_ref[N_FFN_CHUNKS + j], preferred_element_type=F32)
        val = conv3(uv, j)
        gate = conv3(ug, N_FFN_CHUNKS + j)
        pbuf[:, j * FFN_CHUNK:(j + 1) * FFN_CHUNK] = (_gelu(gate) * val).astype(BF16)
    for r0 in range(0, rows, FFN_OUT_ROWS):
        f = jnp.dot(pbuf[r0:r0 + FFN_OUT_ROWS, :], wd_ref[...], preferred_element_type=F32)
        x1 = xbuf[fh + r0:fh + r0 + FFN_OUT_ROWS, :]
        y = _ln(DEEPNORM_ALPHA * x1 + f, lg_ref[...], lb_ref[...])
        if last:
            t0 = r0 // NV
            o_ref[:, t0:t0 + FFN_OUT_ROWS // NV, :] = _to_natural(y)
        else:
            o_ref[r0:r0 + FFN_OUT_ROWS, :] = y


def _ffn(x, p, chunks, last):
    rows_total = x.shape[0]
    rows = FFN_TILE_ROWS
    small = [p["ln1_g"], p["ln1_b"], p["w_up"], p["ffn_dw_w"], p["ffn_dw_b"], p["w_down"],
             p["ln2_g"], p["ln2_b"]]
    if last:
        out_spec = pl.BlockSpec((NV, rows // NV, D_MODEL), lambda i: (0, i, 0))
        out_shape = jax.ShapeDtypeStruct((NV, rows_total // NV, D_MODEL), F32)
    else:
        out_spec = pl.BlockSpec((rows, D_MODEL), lambda i: (i, 0))
        out_shape = jax.ShapeDtypeStruct((rows_total, D_MODEL), F32)
    return pl.pallas_call(
        functools.partial(_ffn_body, rows=rows, chunks=chunks, last=last),
        grid=(rows_total // rows,),
        in_specs=_halo_specs(rows, FFN_HALO_ROWS, D_MODEL, rows_total)
        + [_const_spec(a.shape) for a in small],
        out_specs=out_spec,
        out_shape=out_shape,
        scratch_shapes=[
            pltpu.VMEM((rows + 2 * FFN_HALO_ROWS, D_MODEL), F32),
            pltpu.VMEM((rows, D_FF), BF16),
        ],
        compiler_params=_params(),
        name="ffn",
    )(x, x, x, *small)


def _block_diag_gate(w_a, w_x):
    heads_per_half = RNN_HEADS // 2
    half_w = D_RNN // 2

    def bd(w):
        eye = jnp.eye(heads_per_half, dtype=w.dtype)
        return jnp.einsum("hij,hg->higj", w, eye).reshape(half_w, half_w)

    out = []
    for d in range(2):
        halves = []
        for hf in range(2):
            sl = slice(hf * heads_per_half, (hf + 1) * heads_per_half)
            halves.append(jnp.concatenate([bd(w_a[d, sl]), bd(w_x[d, sl])], axis=-1))
        out.append(jnp.stack(halves))
    return jnp.stack(out).astype(BF16)


def _layer_params(l, w_in, conv_dw_w, conv_dw_b, conv_ln_g, conv_ln_b, rnn_conv_w, rnn_conv_b,
                  rg_w_a, rg_b_a, rg_w_x, rg_b_x, rg_lambda, w_out, ln1_g, ln1_b, w_up,
                  ffn_dw_w, ffn_dw_b, w_down, ln2_g, ln2_b):
    nc = N_FFN_CHUNKS
    row = lambda a: a.reshape(1, -1)
    return {
        "w_in": w_in[l].astype(BF16),
        "conv_w": jnp.repeat(conv_dw_w[l], NV, axis=0),
        "conv_b": jnp.broadcast_to(row(conv_dw_b[l]), (NV, D_CONV)),
        "conv_ln_g": row(conv_ln_g[l]), "conv_ln_b": row(conv_ln_b[l]),
        "rnn_conv_w": rnn_conv_w[l].reshape(2 * RNN_CONV_WIDTH, D_RNN),
        "rnn_conv_b": rnn_conv_b[l],
        "w_gate": _block_diag_gate(rg_w_a[l], rg_w_x[l]),
        "b_a": rg_b_a[l], "b_x": rg_b_x[l], "lam": rg_lambda[l],
        "w_out": w_out[l].astype(BF16), "ln1_g": row(ln1_g[l]), "ln1_b": row(ln1_b[l]),
        "w_up": w_up[l].astype(BF16).reshape(D_MODEL, 2 * nc, FFN_CHUNK).transpose(1, 0, 2),
        "ffn_dw_w": ffn_dw_w[l].reshape(3, 2 * nc, FFN_CHUNK).transpose(1, 0, 2),
        "ffn_dw_b": ffn_dw_b[l].reshape(2 * nc, 1, FFN_CHUNK),
        "w_down": w_down[l].astype(BF16),
        "ln2_g": row(ln2_g[l]), "ln2_b": row(ln2_b[l]),
    }


def _trunk(x, ln_in_g, ln_in_b, layers):
    bsz, seq, _ = x.shape
    chunks = NV // bsz
    sv = seq // chunks
    xt = x.reshape(NV, sv, D_MODEL)
    ln_g = ln_in_g.reshape(1, -1)
    ln_b = ln_in_b.reshape(1, -1)
    for l, p in enumerate(layers):
        outs = _proj_rnn(xt, ln_g, ln_b, p, chunks, first=(l == 0))
        if l == 0:
            xt, outs = outs[0], outs[1:]
        g, gr, s, pb, hb0, pb0 = outs[:6]
        cb, qb, eb = _carry(hb0, pb0)
        extra = (outs[6], qb, outs[7], eb) if chunks > 1 else None
        x1 = _mix_out(xt, g, gr, s, pb, cb, extra, p, chunks)
        xt = _ffn(x1, p, chunks, last=(l == len(layers) - 1))
    return xt.reshape(bsz, seq, D_MODEL)


def kernel(x_prompt, x_sample, ln_in_g, ln_in_b, w_in, conv_dw_w, conv_dw_b, conv_ln_g,
           conv_ln_b, rnn_conv_w, rnn_conv_b, rg_w_a, rg_b_a, rg_w_x, rg_b_x, rg_lambda,
           w_out, ln1_g, ln1_b, w_up, ffn_dw_w, ffn_dw_b, w_down, ln2_g, ln2_b):
    layers = [
        _layer_params(l, w_in, conv_dw_w, conv_dw_b, conv_ln_g, conv_ln_b, rnn_conv_w,
                      rnn_conv_b, rg_w_a, rg_b_a, rg_w_x, rg_b_x, rg_lambda, w_out, ln1_g,
                      ln1_b, w_up, ffn_dw_w, ffn_dw_b, w_down, ln2_g, ln2_b)
        for l in range(DEPTH)
    ]
    y_prompt = _trunk(x_prompt, ln_in_g, ln_in_b, layers)
    y_sample = _trunk(x_sample, ln_in_g, ln_in_b, layers)
    return (y_prompt, y_sample)
```

```python
import functools

import jax
import jax.numpy as jnp
from jax import lax
from jax.experimental import pallas as pl
from jax.experimental.pallas import tpu as pltpu

D_MODEL = 1024
D_CONV = 512
D_RNN = 512
D_IN = 2 * D_CONV + 2 * D_RNN
CONV_WIDTH = 31
RNN_HEADS = 8
RNN_CONV_WIDTH = 4
LRU_C = 8.0
D_FF = 2816
FFN_CHUNK = 256
N_FFN_CHUNKS = D_FF // FFN_CHUNK
LN_EPS = 1e-5
DEPTH = 2
DEEPNORM_ALPHA = (2.0 * DEPTH) ** 0.25

NV = 8
CONV_HALO_ROWS = 128
RNN_HALO_ROWS = 32
FFN_HALO_ROWS = NV
TILE_ROWS = 512
TILE_STEPS = TILE_ROWS // NV
FFN_TILE_ROWS = 1024
FFN_OUT_ROWS = 256
NAT_HALO_STEPS = 8
CONV_ROW_CHUNK = 64
LANES = 128
VMEM_LIMIT = 56 * 1024 * 1024

F32 = jnp.float32
BF16 = jnp.bfloat16


def _ln(x, g, b):
    mu = jnp.mean(x, axis=-1, keepdims=True)
    xc = x - mu
    var = jnp.mean(xc * xc, axis=-1, keepdims=True)
    return xc * lax.rsqrt(var + LN_EPS) * g + b


def _times_sigmoid(v, x):
    half_v = 0.5 * v
    return half_v * jnp.tanh(0.5 * x) + half_v


GELU_C = (2.0 / 3.141592653589793) ** 0.5


def _gelu(x):
    inner = x * (GELU_C + (GELU_C * 0.044715) * (x * x))
    half_x = 0.5 * x
    return half_x + half_x * jnp.tanh(inner)


def _const_spec(shape):
    nd = len(shape)
    return pl.BlockSpec(shape, lambda *_: (0,) * nd, pipeline_mode=pl.Buffered(1))


def _params():
    return pltpu.CompilerParams(
        dimension_semantics=("arbitrary",), vmem_limit_bytes=VMEM_LIMIT)


def _to_time_major(x3):
    return pltpu.einshape("vtd->tvd", x3).reshape(x3.shape[1] * NV, x3.shape[2])


def _to_natural(x2):
    return pltpu.einshape("tvd->vtd", x2.reshape(x2.shape[0] // NV, NV, x2.shape[1]))


def _edge_halos(prev, nxt, chunks):
    halo = prev.shape[0]
    if chunks == 1:
        return jnp.zeros_like(prev), jnp.zeros_like(nxt)
    piece = lax.broadcasted_iota(jnp.int32, prev.shape, 0) & (chunks - 1)
    prev_edge = jnp.where(piece != 0, pltpu.roll(prev, 1, axis=0), 0.0)
    next_edge = jnp.where(piece != chunks - 1, pltpu.roll(nxt, halo - 1, axis=0), 0.0)
    return prev_edge, next_edge


def _fill_with_halo(buf, prev, main, nxt, chunks, is_first=None, is_last=None):
    halo = prev.shape[0]
    rows = main.shape[0]
    if is_first is None:
        is_first = pl.program_id(0) == 0
        is_last = pl.program_id(0) == pl.num_programs(0) - 1
    prev_edge, next_edge = _edge_halos(prev, nxt, chunks)
    buf[0:halo, :] = jnp.where(is_first, prev_edge, prev)
    buf[halo:halo + rows, :] = main
    buf[halo + rows:, :] = jnp.where(is_last, next_edge, nxt)


def _halo_specs(rows, halo, width, n_rows_total, tile=lambda i: i):
    per = rows // halo
    nblk = n_rows_total // halo
    return [
        pl.BlockSpec((halo, width), lambda i: ((tile(i) * per + nblk - 1) % nblk, 0)),
        pl.BlockSpec((rows, width), lambda i: (tile(i), 0)),
        pl.BlockSpec((halo, width), lambda i: (((tile(i) + 1) * per) % nblk, 0)),
    ]


def _natural_halo_specs(steps, width, n_steps_total, tile=lambda i: i):
    hs = NAT_HALO_STEPS
    per = steps // hs
    nblk = n_steps_total // hs
    return [
        pl.BlockSpec((NV, hs, width), lambda i: (0, (tile(i) * per + nblk - 1) % nblk, 0)),
        pl.BlockSpec((NV, steps, width), lambda i: (0, tile(i), 0)),
        pl.BlockSpec((NV, hs, width), lambda i: (0, ((tile(i) + 1) * per) % nblk, 0)),
    ]


def _proj_rnn_body(*refs, rows, chunks, first):
    (xp_ref, x_ref, xn_ref, lg_ref, lb_ref, w_ref,
     rw_ref, rb_ref, wg_ref, ba_ref, bx_ref, lam_ref) = refs[:12]
    outs = list(refs[12:])
    xo_ref = outs.pop(0) if first else None
    g_ref, gr_ref, s_ref, pb_ref, hb0_ref, pb0_ref = outs[:6]
    outs = outs[6:]
    if chunks > 1:
        qf_ref, ef_ref = outs[:2]
        outs = outs[2:]
    xbuf, rbuf, a_buf, b_buf, sf = outs[:5]
    sq = outs[5] if chunks > 1 else None

    halo = RNN_HALO_ROWS
    i = pl.program_id(0)

    if first:
        nat_halo = NAT_HALO_STEPS * NV
        xbuf[0:halo, :] = _to_time_major(xp_ref[...])[nat_halo - halo:]
        xbuf[halo:halo + rows, :] = _to_time_major(x_ref[...])
        xbuf[halo + rows:, :] = _to_time_major(xn_ref[...])[0:halo]
        x = _ln(xbuf[...], lg_ref[...], lb_ref[...])
        xo_ref[...] = x[halo:halo + rows]
    else:
        xbuf[0:halo, :] = xp_ref[...]
        xbuf[halo:halo + rows, :] = x_ref[...]
        xbuf[halo + rows:, :] = xn_ref[...]
        x = xbuf[...]
    xb16 = x.astype(BF16)
    xm16 = xb16[halo:halo + rows]
    rx = jnp.dot(xb16, w_ref[:, 2 * D_CONV:2 * D_CONV + D_RNN], preferred_element_type=F32)
    _fill_with_halo(rbuf, rx[0:halo], rx[halo:halo + rows], rx[halo + rows:], chunks)

    half_w = D_RNN // 2
    for d in range(2):
        if d == 0:
            hc = jnp.dot(xm16, w_ref[:, 0:2 * D_CONV], preferred_element_type=F32)
            g_ref[...] = _times_sigmoid(hc[:, :D_CONV], hc[:, D_CONV:])
        else:
            hg = jnp.dot(xm16, w_ref[:, 2 * D_CONV + D_RNN:], preferred_element_type=F32)
            gr_ref[...] = _gelu(hg)
        xd = jnp.broadcast_to(rb_ref[d:d + 1, :], (rows, D_RNN))
        for k in range(RNN_CONV_WIDTH):
            shift = (k - (RNN_CONV_WIDTH - 1)) if d == 0 else k
            start = halo + shift * NV
            tap = d * RNN_CONV_WIDTH + k
            xd = xd + rbuf[start:start + rows, :] * rw_ref[tap:tap + 1, :]
        xb = xd.astype(BF16)
        parts = [jnp.dot(xb[:, hf * half_w:(hf + 1) * half_w], wg_ref[d, hf],
                         preferred_element_type=F32) for hf in range(2)]
        gate_a = jnp.concatenate([p[:, :half_w] for p in parts], axis=-1) + ba_ref[d:d + 1, :]
        gate_x = jnp.concatenate([p[:, half_w:] for p in parts], axis=-1) + bx_ref[d:d + 1, :]
        z = -lam_ref[d:d + 1, :]
        softplus = jnp.maximum(z, 0.0) + jnp.log1p(jnp.exp(-jnp.abs(z)))
        half_decay = (-0.5 * LRU_C) * softplus
        log_a = half_decay * jnp.tanh(0.5 * gate_a) + half_decay
        a = jnp.exp(log_a)
        one_minus_a2 = -jnp.tanh(log_a) * (a * a + 1.0)
        root = jnp.where(one_minus_a2 > 0.0, one_minus_a2 * lax.rsqrt(one_minus_a2), 0.0)
        a_buf[d] = a
        b_buf[d] = root * _times_sigmoid(xd, gate_x)

    @pl.when(i == 0)
    def _():
        sf[...] = jnp.zeros_like(sf)
        if chunks > 1:
            sq[...] = jnp.ones_like(sq)

    nsteps = rows // NV

    def make_step(accumulate):
        def step(t, carry):
            hf, qf, hb, pb = carry
            rf = pl.multiple_of(t * NV, NV)
            rb = pl.multiple_of((nsteps - 1 - t) * NV, NV)
            af = a_buf[0, pl.ds(rf, NV), :]
            hf = af * hf + b_buf[0, pl.ds(rf, NV), :]
            ab = a_buf[1, pl.ds(rb, NV), :]
            hb = ab * hb + b_buf[1, pl.ds(rb, NV), :]
            pb = ab * pb
            pb_ref[pl.ds(rb, NV), :] = pb
            if chunks > 1:
                qf = af * qf
                qf_ref[pl.ds(rf, NV), :] = qf
            if accumulate:
                s_ref[pl.ds(rf, NV), :] += hf
                s_ref[pl.ds(rb, NV), :] += hb
            else:
                s_ref[pl.ds(rf, NV), :] = hf
                s_ref[pl.ds(rb, NV), :] = hb
            return hf, qf, hb, pb
        return step

    zeros = jnp.zeros((NV, D_RNN), F32)
    ones = jnp.ones((NV, D_RNN), F32)
    q0 = sq[...] if chunks > 1 else ones
    carry = (sf[...], q0, zeros, ones)
    carry = lax.fori_loop(0, nsteps // 2, make_step(False), carry, unroll=8)
    hf, qf, hb, pb = lax.fori_loop(nsteps // 2, nsteps, make_step(True), carry, unroll=8)
    sf[...] = hf
    hb0_ref[...] = hb
    pb0_ref[...] = pb
    if chunks > 1:
        sq[...] = qf
        ef_ref[...] = hf


def _proj_rnn(x, ln_g, ln_b, p, chunks, first):
    rows = TILE_ROWS
    halo = RNN_HALO_ROWS
    if first:
        rows_total = x.shape[1] * NV
        x_specs = _natural_halo_specs(TILE_STEPS, D_MODEL, x.shape[1])
    else:
        rows_total = x.shape[0]
        x_specs = _halo_specs(rows, halo, D_MODEL, rows_total)
    nt = rows_total // rows
    small = [ln_g, ln_b, p["w_in"], p["rnn_conv_w"], p["rnn_conv_b"], p["w_gate"],
             p["b_a"], p["b_x"], p["lam"]]
    blk = (NV, D_RNN)
    wide = pl.BlockSpec((rows, D_MODEL), lambda i: (i, 0))
    narrow = pl.BlockSpec((rows, D_RNN), lambda i: (i, 0))
    per_tile = pl.BlockSpec(blk, lambda i: (i, 0))
    big = jax.ShapeDtypeStruct((rows_total, D_RNN), F32)
    summary = jax.ShapeDtypeStruct((nt * NV, D_RNN), F32)
    out_specs = ([wide] if first else []) + [narrow] * 4 + [per_tile] * 2
    out_shape = ([jax.ShapeDtypeStruct((rows_total, D_MODEL), F32)] if first else []) \
        + [big] * 4 + [summary] * 2
    scratch = [
        pltpu.VMEM((rows + 2 * halo, D_MODEL), F32),
        pltpu.VMEM((rows + 2 * halo, D_RNN), F32),
        pltpu.VMEM((2, rows, D_RNN), F32),
        pltpu.VMEM((2, rows, D_RNN), F32),
        pltpu.VMEM(blk, F32),
    ]
    if chunks > 1:
        out_specs += [narrow, pl.BlockSpec(blk, lambda i: (0, 0))]
        out_shape += [big, jax.ShapeDtypeStruct(blk, F32)]
        scratch.append(pltpu.VMEM(blk, F32))
    return pl.pallas_call(
        functools.partial(_proj_rnn_body, rows=rows, chunks=chunks, first=first),
        grid=(nt,),
        in_specs=x_specs + [_const_spec(a.shape) for a in small],
        out_specs=out_specs,
        out_shape=out_shape,
        scratch_shapes=scratch,
        compiler_params=_params(),
        name="proj_rnn",
    )(x, x, x, *small)


def _carry_body(hb0_ref, pb0_ref, cb_ref, qb_ref, eb_ref, *, nt):
    def step(k, carry):
        sb, ub = carry
        r = pl.multiple_of((nt - 1 - k) * NV, NV)
        cb_ref[pl.ds(r, NV), :] = sb
        qb_ref[pl.ds(r, NV), :] = ub
        p0 = pb0_ref[pl.ds(r, NV), :]
        return hb0_ref[pl.ds(r, NV), :] + p0 * sb, p0 * ub

    init = (jnp.zeros((NV, D_RNN), F32), jnp.ones((NV, D_RNN), F32))
    sb, _ = lax.fori_loop(0, nt, step, init)
    eb_ref[...] = sb


def _carry(hb0, pb0):
    nt = hb0.shape[0] // NV
    return pl.pallas_call(
        functools.partial(_carry_body, nt=nt),
        out_shape=[jax.ShapeDtypeStruct((nt * NV, D_RNN), F32)] * 2
        + [jax.ShapeDtypeStruct((NV, D_RNN), F32)],
        compiler_params=pltpu.CompilerParams(vmem_limit_bytes=VMEM_LIMIT),
        name="carry",
    )(hb0, pb0)


def _mix_out_body(*refs, rows, chunks):
    (gp_ref, g_ref, gn_ref, x_ref, s_ref, pb_ref, gr_ref, cb_ref) = refs[:8]
    refs = refs[8:]
    if chunks > 1:
        qf_ref, qb_ref, ef_ref, eb_ref = refs[:4]
        refs = refs[4:]
    (cw_ref, cb_bias_ref, clg_ref, clb_ref, wo_ref, o_ref, gbuf, cbuf) = refs

    halo = CONV_HALO_ROWS
    i = pl.program_id(0)
    gp = gp_ref[...]
    gn = gn_ref[...]
    gp_edge, gn_edge = _edge_halos(gp, gn, chunks)
    gp = jnp.where(i == 0, gp_edge, gp)
    gn = jnp.where(i == pl.num_programs(0) - 1, gn_edge, gn)
    for col in range(D_CONV // LANES):
        lanes = slice(col * LANES, (col + 1) * LANES)
        gbuf[col, 0:halo, :] = gp[:, lanes]
        gbuf[col, halo:halo + rows, :] = g_ref[:, lanes]
        gbuf[col, halo + rows:, :] = gn[:, lanes]

    half = CONV_WIDTH // 2

    conv_shape = (CONV_ROW_CHUNK // NV, NV, LANES)
    for col in range(D_CONV // LANES):
        lanes = slice(col * LANES, (col + 1) * LANES)
        taps = [cw_ref[k * NV:(k + 1) * NV, lanes][None] for k in range(CONV_WIDTH)]
        bias = jnp.broadcast_to(cb_bias_ref[:, lanes][None], conv_shape)

        def conv_chunk(j, _, col=col, lanes=lanes, taps=taps, bias=bias):
            base = pl.multiple_of(j * CONV_ROW_CHUNK, CONV_ROW_CHUNK)
            acc = [bias, None]
            for k in range(CONV_WIDTH):
                start = base + (halo + (k - half) * NV)
                term = gbuf[col, pl.ds(start, CONV_ROW_CHUNK), :].reshape(conv_shape) * taps[k]
                acc[k % 2] = term if acc[k % 2] is None else acc[k % 2] + term
            total = acc[0] + acc[1]
            cbuf[pl.ds(base, CONV_ROW_CHUNK), lanes] = total.reshape(CONV_ROW_CHUNK, LANES)
            return 0

        lax.fori_loop(0, rows // CONV_ROW_CHUNK, conv_chunk, 0)
    cn = _ln(cbuf[...], clg_ref[...], clb_ref[...])
    c = _times_sigmoid(cn, cn)

    nsteps = rows // NV
    shape3 = (nsteps, NV, D_RNN)
    cb = cb_ref[...]
    rec = s_ref[...].reshape(shape3)
    if chunks > 1:
        ef, eb = _edge_halos(ef_ref[...], eb_ref[...], chunks)
        cb = cb + qb_ref[...] * eb
        rec = rec + qf_ref[...].reshape(shape3) * ef[None]
    rec = rec + pb_ref[...].reshape(shape3) * cb[None]
    rec = rec.reshape(rows, D_RNN) * gr_ref[...]

    mix = (jnp.dot(c.astype(BF16), wo_ref[0:D_CONV, :], preferred_element_type=F32)
           + jnp.dot(rec.astype(BF16), wo_ref[D_CONV:, :], preferred_element_type=F32))
    o_ref[...] = DEEPNORM_ALPHA * x_ref[...] + mix


def _mix_out(x, g, gr, s, pb, cb, extra, p, chunks):
    rows_total = x.shape[0]
    rows = TILE_ROWS
    halo = CONV_HALO_ROWS
    blk = (NV, D_RNN)
    wide = pl.BlockSpec((rows, D_MODEL), lambda i: (i, 0))
    narrow = pl.BlockSpec((rows, D_RNN), lambda i: (i, 0))
    per_tile = pl.BlockSpec(blk, lambda i: (i, 0))
    small = [p["conv_w"], p["conv_b"], p["conv_ln_g"], p["conv_ln_b"], p["w_out"]]
    in_specs = _halo_specs(rows, halo, D_CONV, rows_total) + [wide] + [narrow] * 3 + [per_tile]
    args = [g, g, g, x, s, pb, gr, cb]
    if chunks > 1:
        qf, qb, ef, eb = extra
        in_specs += [narrow, per_tile, _const_spec(blk), _const_spec(blk)]
        args += [qf, qb, ef, eb]
    return pl.pallas_call(
        functools.partial(_mix_out_body, rows=rows, chunks=chunks),
        grid=(rows_total // rows,),
        in_specs=in_specs + [_const_spec(a.shape) for a in small],
        out_specs=wide,
        out_shape=jax.ShapeDtypeStruct((rows_total, D_MODEL), F32),
        scratch_shapes=[
            pltpu.VMEM((D_CONV // LANES, rows + 2 * halo, LANES), F32),
            pltpu.VMEM((rows, D_CONV), F32),
        ],
        compiler_params=_params(),
        name="mix_out",
    )(*args, *small)


def _ffn_body(xp_ref, x_ref, xn_ref, l1g_ref, l1b_ref, wu_ref, dw_ref, db_ref, wd_ref,
              lg_ref, lb_ref, o_ref, xbuf, pbuf, *, rows, chunks, last):
    norm1 = lambda ref: _ln(ref[...], l1g_ref[...], l1b_ref[...])
    _fill_with_halo(xbuf, norm1(xp_ref), norm1(x_ref), norm1(xn_ref), chunks)
    fh = FFN_HALO_ROWS
    xe = xbuf[...].astype(BF16)

    def conv3(u, idx):
        w = dw_ref[idx]
        return (u[0:rows] * w[0:1, :] + u[NV:NV + rows] * w[1:2, :]
                + u[2 * NV:2 * NV + rows] * w[2:3, :] + db_ref[idx])

    for j in range(N_FFN_CHUNKS):
        uv = jnp.dot(xe, wu_ref[j], preferred_element_type=F32)
        ug = jnp.dot(xe, wu_ref[N_FFN_CHUNKS + j], preferred_element_type=F32)
        val = conv3(uv, j)
        gate = conv3(ug, N_FFN_CHUNKS + j)
        pbuf[:, j * FFN_CHUNK:(j + 1) * FFN_CHUNK] = (_gelu(gate) * val).astype(BF16)
    for r0 in range(0, rows, FFN_OUT_ROWS):
        f = jnp.dot(pbuf[r0:r0 + FFN_OUT_ROWS, :], wd_ref[...], preferred_element_type=F32)
        x1 = xbuf[fh + r0:fh + r0 + FFN_OUT_ROWS, :]
        y = _ln(DEEPNORM_ALPHA * x1 + f, lg_ref[...], lb_ref[...])
        if last:
            t0 = r0 // NV
            o_ref[:, t0:t0 + FFN_OUT_ROWS // NV, :] = _to_natural(y)
        else:
            o_ref[r0:r0 + FFN_OUT_ROWS, :] = y


def _ffn(x, p, chunks, last):
    rows_total = x.shape[0]
    rows = FFN_TILE_ROWS
    small = [p["ln1_g"], p["ln1_b"], p["w_up"], p["ffn_dw_w"], p["ffn_dw_b"], p["w_down"],
             p["ln2_g"], p["ln2_b"]]
    if last:
        out_spec = pl.BlockSpec((NV, rows // NV, D_MODEL), lambda i: (0, i, 0))
        out_shape = jax.ShapeDtypeStruct((NV, rows_total // NV, D_MODEL), F32)
    else:
        out_spec = pl.BlockSpec((rows, D_MODEL), lambda i: (i, 0))
        out_shape = jax.ShapeDtypeStruct((rows_total, D_MODEL), F32)
    return pl.pallas_call(
        functools.partial(_ffn_body, rows=rows, chunks=chunks, last=last),
        grid=(rows_total // rows,),
        in_specs=_halo_specs(rows, FFN_HALO_ROWS, D_MODEL, rows_total)
        + [_const_spec(a.shape) for a in small],
        out_specs=out_spec,
        out_shape=out_shape,
        scratch_shapes=[
            pltpu.VMEM((rows + 2 * FFN_HALO_ROWS, D_MODEL), F32),
            pltpu.VMEM((rows, D_FF), BF16),
        ],
        compiler_params=_params(),
        name="ffn",
    )(x, x, x, *small)


def _block_diag_gate(w_a, w_x):
    heads_per_half = RNN_HEADS // 2
    half_w = D_RNN // 2

    def bd(w):
        eye = jnp.eye(heads_per_half, dtype=w.dtype)
        return jnp.einsum("hij,hg->higj", w, eye).reshape(half_w, half_w)

    out = []
    for d in range(2):
        halves = []
        for hf in range(2):
            sl = slice(hf * heads_per_half, (hf + 1) * heads_per_half)
            halves.append(jnp.concatenate([bd(w_a[d, sl]), bd(w_x[d, sl])], axis=-1))
        out.append(jnp.stack(halves))
    return jnp.stack(out).astype(BF16)


def _layer_params(l, w_in, conv_dw_w, conv_dw_b, conv_ln_g, conv_ln_b, rnn_conv_w, rnn_conv_b,
                  rg_w_a, rg_b_a, rg_w_x, rg_b_x, rg_lambda, w_out, ln1_g, ln1_b, w_up,
                  ffn_dw_w, ffn_dw_b, w_down, ln2_g, ln2_b):
    nc = N_FFN_CHUNKS
    row = lambda a: a.reshape(1, -1)
    return {
        "w_in": w_in[l].astype(BF16),
        "conv_w": jnp.repeat(conv_dw_w[l], NV, axis=0),
        "conv_b": jnp.broadcast_to(row(conv_dw_b[l]), (NV, D_CONV)),
        "conv_ln_g": row(conv_ln_g[l]), "conv_ln_b": row(conv_ln_b[l]),
        "rnn_conv_w": rnn_conv_w[l].reshape(2 * RNN_CONV_WIDTH, D_RNN),
        "rnn_conv_b": rnn_conv_b[l],
        "w_gate": _block_diag_gate(rg_w_a[l], rg_w_x[l]),
        "b_a": rg_b_a[l], "b_x": rg_b_x[l], "lam": rg_lambda[l],
        "w_out": w_out[l].astype(BF16), "ln1_g": row(ln1_g[l]), "ln1_b": row(ln1_b[l]),
        "w_up": w_up[l].astype(BF16).reshape(D_MODEL, 2 * nc, FFN_CHUNK).transpose(1, 0, 2),
        "ffn_dw_w": ffn_dw_w[l].reshape(3, 2 * nc, FFN_CHUNK).transpose(1, 0, 2),
        "ffn_dw_b": ffn_dw_b[l].reshape(2 * nc, 1, FFN_CHUNK),
        "w_down": w_down[l].astype(BF16),
        "ln2_g": row(ln2_g[l]), "ln2_b": row(ln2_b[l]),
    }


def _trunk(x, ln_in_g, ln_in_b, layers):
    bsz, seq, _ = x.shape
    chunks = NV // bsz
    sv = seq // chunks
    xt = x.reshape(NV, sv, D_MODEL)
    ln_g = ln_in_g.reshape(1, -1)
    ln_b = ln_in_b.reshape(1, -1)
    for l, p in enumerate(layers):
        outs = _proj_rnn(xt, ln_g, ln_b, p, chunks, first=(l == 0))
        if l == 0:
            xt, outs = outs[0], outs[1:]
        g, gr, s, pb, hb0, pb0 = outs[:6]
        cb, qb, eb = _carry(hb0, pb0)
        extra = (outs[6], qb, outs[7], eb) if chunks > 1 else None
        x1 = _mix_out(xt, g, gr, s, pb, cb, extra, p, chunks)
        xt = _ffn(x1, p, chunks, last=(l == len(layers) - 1))
    return xt.reshape(bsz, seq, D_MODEL)


def kernel(x_prompt, x_sample, ln_in_g, ln_in_b, w_in, conv_dw_w, conv_dw_b, conv_ln_g,
           conv_ln_b, rnn_conv_w, rnn_conv_b, rg_w_a, rg_b_a, rg_w_x, rg_b_x, rg_lambda,
           w_out, ln1_g, ln1_b, w_up, ffn_dw_w, ffn_dw_b, w_down, ln2_g, ln2_b):
    layers = [
        _layer_params(l, w_in, conv_dw_w, conv_dw_b, conv_ln_g, conv_ln_b, rnn_conv_w,
                      rnn_conv_b, rg_w_a, rg_b_a, rg_w_x, rg_b_x, rg_lambda, w_out, ln1_g,
                      ln1_b, w_up, ffn_dw_w, ffn_dw_b, w_down, ln2_g, ln2_b)
        for l in range(DEPTH)
    ]
    y_prompt = _trunk(x_prompt, ln_in_g, ln_in_b, layers)
    y_sample = _trunk(x_sample, ln_in_g, ln_in_b, layers)
    return (y_prompt, y_sample)
```

```python
import functools

import jax
import jax.numpy as jnp
from jax import lax
from jax.experimental import pallas as pl
from jax.experimental.pallas import tpu as pltpu

D_MODEL = 1024
D_CONV = 512
D_RNN = 512
D_IN = 2 * D_CONV + 2 * D_RNN
CONV_WIDTH = 31
RNN_HEADS = 8
RNN_CONV_WIDTH = 4
LRU_C = 8.0
D_FF = 2816
FFN_CHUNK = 256
N_FFN_CHUNKS = D_FF // FFN_CHUNK
LN_EPS = 1e-5
DEPTH = 2
DEEPNORM_ALPHA = (2.0 * DEPTH) ** 0.25

NV = 8
CONV_HALO_ROWS = 128
RNN_HALO_ROWS = 32
FFN_HALO_ROWS = NV
TILE_ROWS = 512
TILE_STEPS = TILE_ROWS // NV
FFN_TILE_ROWS = 1024
FFN_OUT_ROWS = 256
NAT_HALO_STEPS = 8
CONV_ROW_CHUNK = 64
LANES = 128
VMEM_LIMIT = 56 * 1024 * 1024

F32 = jnp.float32
BF16 = jnp.bfloat16


def _ln(x, g, b):
    mu = jnp.mean(x, axis=-1, keepdims=True)
    xc = x - mu
    var = jnp.mean(xc * xc, axis=-1, keepdims=True)
    return xc * lax.rsqrt(var + LN_EPS) * g + b


def _times_sigmoid(v, x):
    half_v = 0.5 * v
    return half_v * jnp.tanh(0.5 * x) + half_v


GELU_C = (2.0 / 3.141592653589793) ** 0.5


def _gelu(x):
    inner = x * (GELU_C + (GELU_C * 0.044715) * (x * x))
    half_x = 0.5 * x
    return half_x + half_x * jnp.tanh(inner)


def _const_spec(shape):
    nd = len(shape)
    return pl.BlockSpec(shape, lambda *_: (0,) * nd, pipeline_mode=pl.Buffered(1))


def _params():
    return pltpu.CompilerParams(
        dimension_semantics=("arbitrary",), vmem_limit_bytes=VMEM_LIMIT)


def _to_time_major(x3):
    return pltpu.einshape("vtd->tvd", x3).reshape(x3.shape[1] * NV, x3.shape[2])


def _to_natural(x2):
    return pltpu.einshape("tvd->vtd", x2.reshape(x2.shape[0] // NV, NV, x2.shape[1]))


def _edge_halos(prev, nxt, chunks):
    halo = prev.shape[0]
    if chunks == 1:
        return jnp.zeros_like(prev), jnp.zeros_like(nxt)
    piece = lax.broadcasted_iota(jnp.int32, prev.shape, 0) & (chunks - 1)
    prev_edge = jnp.where(piece != 0, pltpu.roll(prev, 1, axis=0), 0.0)
    next_edge = jnp.where(piece != chunks - 1, pltpu.roll(nxt, halo - 1, axis=0), 0.0)
    return prev_edge, next_edge


def _fill_with_halo(buf, prev, main, nxt, chunks, is_first=None, is_last=None):
    halo = prev.shape[0]
    rows = main.shape[0]
    if is_first is None:
        is_first = pl.program_id(0) == 0
        is_last = pl.program_id(0) == pl.num_programs(0) - 1
    prev_edge, next_edge = _edge_halos(prev, nxt, chunks)
    buf[0:halo, :] = jnp.where(is_first, prev_edge, prev)
    buf[halo:halo + rows, :] = main
    buf[halo + rows:, :] = jnp.where(is_last, next_edge, nxt)


def _halo_specs(rows, halo, width, n_rows_total, tile=lambda i: i):
    per = rows // halo
    nblk = n_rows_total // halo
    return [
        pl.BlockSpec((halo, width), lambda i: ((tile(i) * per + nblk - 1) % nblk, 0)),
        pl.BlockSpec((rows, width), lambda i: (tile(i), 0)),
        pl.BlockSpec((halo, width), lambda i: (((tile(i) + 1) * per) % nblk, 0)),
    ]


def _natural_halo_specs(steps, width, n_steps_total, tile=lambda i: i):
    hs = NAT_HALO_STEPS
    per = steps // hs
    nblk = n_steps_total // hs
    return [
        pl.BlockSpec((NV, hs, width), lambda i: (0, (tile(i) * per + nblk - 1) % nblk, 0)),
        pl.BlockSpec((NV, steps, width), lambda i: (0, tile(i), 0)),
        pl.BlockSpec((NV, hs, width), lambda i: (0, ((tile(i) + 1) * per) % nblk, 0)),
    ]


def _proj_rnn_body(*refs, rows, chunks, first):
    (xp_ref, x_ref, xn_ref, lg_ref, lb_ref, w_ref,
     rw_ref, rb_ref, wg_ref, ba_ref, bx_ref, lam_ref) = refs[:12]
    outs = list(refs[12:])
    xo_ref = outs.pop(0) if first else None
    g_ref, gr_ref, s_ref, pb_ref, hb0_ref, pb0_ref = outs[:6]
    outs = outs[6:]
    if chunks > 1:
        qf_ref, ef_ref = outs[:2]
        outs = outs[2:]
    xbuf, rbuf, a_buf, b_buf, sf = outs[:5]
    sq = outs[5] if chunks > 1 else None

    halo = RNN_HALO_ROWS
    i = pl.program_id(0)

    if first:
        nat_halo = NAT_HALO_STEPS * NV
        xbuf[0:halo, :] = _to_time_major(xp_ref[...])[nat_halo - halo:]
        xbuf[halo:halo + rows, :] = _to_time_major(x_ref[...])
        xbuf[halo + rows:, :] = _to_time_major(xn_ref[...])[0:halo]
        x = _ln(xbuf[...], lg_ref[...], lb_ref[...])
        xo_ref[...] = x[halo:halo + rows]
    else:
        xbuf[0:halo, :] = xp_ref[...]
        xbuf[halo:halo + rows, :] = x_ref[...]
        xbuf[halo + rows:, :] = xn_ref[...]
        x = xbuf[...]
    xb16 = x.astype(BF16)
    xm16 = xb16[halo:halo + rows]
    rx = jnp.dot(xb16, w_ref[:, 2 * D_CONV:2 * D_CONV + D_RNN], preferred_element_type=F32)
    _fill_with_halo(rbuf, rx[0:halo], rx[halo:halo + rows], rx[halo + rows:], chunks)

    half_w = D_RNN // 2
    for d in range(2):
        if d == 0:
            hc = jnp.dot(xm16, w_ref[:, 0:2 * D_CONV], preferred_element_type=F32)
            g_ref[...] = _times_sigmoid(hc[:, :D_CONV], hc[:, D_CONV:])
        else:
            hg = jnp.dot(xm16, w_ref[:, 2 * D_CONV + D_RNN:], preferred_element_type=F32)
            gr_ref[...] = _gelu(hg)
        xd = jnp.broadcast_to(rb_ref[d:d + 1, :], (rows, D_RNN))
        for k in range(RNN_CONV_WIDTH):
            shift = (k - (RNN_CONV_WIDTH - 1)) if d == 0 else k
            start = halo + shift * NV
            tap = d * RNN_CONV_WIDTH + k
            xd = xd + rbuf[start:start + rows, :] * rw_ref[tap:tap + 1, :]
        xb = xd.astype(BF16)
        parts = [jnp.dot(xb[:, hf * half_w:(hf + 1) * half_w], wg_ref[d, hf],
                         preferred_element_type=F32) for hf in range(2)]
        gate_a = jnp.concatenate([p[:, :half_w] for p in parts], axis=-1) + ba_ref[d:d + 1, :]
        gate_x = jnp.concatenate([p[:, half_w:] for p in parts], axis=-1) + bx_ref[d:d + 1, :]
        z = -lam_ref[d:d + 1, :]
        softplus = jnp.maximum(z, 0.0) + jnp.log1p(jnp.exp(-jnp.abs(z)))
        half_decay = (-0.5 * LRU_C) * softplus
        log_a = half_decay * jnp.tanh(0.5 * gate_a) + half_decay
        a = jnp.exp(log_a)
        one_minus_a2 = -jnp.tanh(log_a) * (a * a + 1.0)
        root = jnp.where(one_minus_a2 > 0.0, one_minus_a2 * lax.rsqrt(one_minus_a2), 0.0)
        a_buf[d] = a
        b_buf[d] = root * _times_sigmoid(xd, gate_x)

    @pl.when(i == 0)
    def _():
        sf[...] = jnp.zeros_like(sf)
        if chunks > 1:
            sq[...] = jnp.ones_like(sq)

    nsteps = rows // NV

    def make_step(accumulate):
        def step(t, carry):
            hf, qf, hb, pb = carry
            rf = pl.multiple_of(t * NV, NV)
            rb = pl.multiple_of((nsteps - 1 - t) * NV, NV)
            af = a_buf[0, pl.ds(rf, NV), :]
            hf = af * hf + b_buf[0, pl.ds(rf, NV), :]
            ab = a_buf[1, pl.ds(rb, NV), :]
            hb = ab * hb + b_buf[1, pl.ds(rb, NV), :]
            pb = ab * pb
            pb_ref[pl.ds(rb, NV), :] = pb
            if chunks > 1:
                qf = af * qf
                qf_ref[pl.ds(rf, NV), :] = qf
            if accumulate:
                s_ref[pl.ds(rf, NV), :] += hf
                s_ref[pl.ds(rb, NV), :] += hb
            else:
                s_ref[pl.ds(rf, NV), :] = hf
                s_ref[pl.ds(rb, NV), :] = hb
            return hf, qf, hb, pb
        return step

    zeros = jnp.zeros((NV, D_RNN), F32)
    ones = jnp.ones((NV, D_RNN), F32)
    q0 = sq[...] if chunks > 1 else ones
    carry = (sf[...], q0, zeros, ones)
    carry = lax.fori_loop(0, nsteps // 2, make_step(False), carry, unroll=8)
    hf, qf, hb, pb = lax.fori_loop(nsteps // 2, nsteps, make_step(True), carry, unroll=8)
    sf[...] = hf
    hb0_ref[...] = hb
    pb0_ref[...] = pb
    if chunks > 1:
        sq[...] = qf
        ef_ref[...] = hf


def _proj_rnn(x, ln_g, ln_b, p, chunks, first):
    rows = TILE_ROWS
    halo = RNN_HALO_ROWS
    if first:
        rows_total = x.shape[1] * NV
        x_specs = _natural_halo_specs(TILE_STEPS, D_MODEL, x.shape[1])
    else:
        rows_total = x.shape[0]
        x_specs = _halo_specs(rows, halo, D_MODEL, rows_total)
    nt = rows_total // rows
    small = [ln_g, ln_b, p["w_in"], p["rnn_conv_w"], p["rnn_conv_b"], p["w_gate"],
             p["b_a"], p["b_x"], p["lam"]]
    blk = (NV, D_RNN)
    wide = pl.BlockSpec((rows, D_MODEL), lambda i: (i, 0))
    narrow = pl.BlockSpec((rows, D_RNN), lambda i: (i, 0))
    per_tile = pl.BlockSpec(blk, lambda i: (i, 0))
    big = jax.ShapeDtypeStruct((rows_total, D_RNN), F32)
    summary = jax.ShapeDtypeStruct((nt * NV, D_RNN), F32)
    out_specs = ([wide] if first else []) + [narrow] * 4 + [per_tile] * 2
    out_shape = ([jax.ShapeDtypeStruct((rows_total, D_MODEL), F32)] if first else []) \
        + [big] * 4 + [summary] * 2
    scratch = [
        pltpu.VMEM((rows + 2 * halo, D_MODEL), F32),
        pltpu.VMEM((rows + 2 * halo, D_RNN), F32),
        pltpu.VMEM((2, rows, D_RNN), F32),
        pltpu.VMEM((2, rows, D_RNN), F32),
        pltpu.VMEM(blk, F32),
    ]
    if chunks > 1:
        out_specs += [narrow, pl.BlockSpec(blk, lambda i: (0, 0))]
        out_shape += [big, jax.ShapeDtypeStruct(blk, F32)]
        scratch.append(pltpu.VMEM(blk, F32))
    return pl.pallas_call(
        functools.partial(_proj_rnn_body, rows=rows, chunks=chunks, first=first),
        grid=(nt,),
        in_specs=x_specs + [_const_spec(a.shape) for a in small],
        out_specs=out_specs,
        out_shape=out_shape,
        scratch_shapes=scratch,
        compiler_params=_params(),
        name="proj_rnn",
    )(x, x, x, *small)


def _carry_body(hb0_ref, pb0_ref, cb_ref, qb_ref, eb_ref, *, nt):
    def step(k, carry):
        sb, ub = carry
        r = pl.multiple_of((nt - 1 - k) * NV, NV)
        cb_ref[pl.ds(r, NV), :] = sb
        qb_ref[pl.ds(r, NV), :] = ub
        p0 = pb0_ref[pl.ds(r, NV), :]
        return hb0_ref[pl.ds(r, NV), :] + p0 * sb, p0 * ub

    init = (jnp.zeros((NV, D_RNN), F32), jnp.ones((NV, D_RNN), F32))
    sb, _ = lax.fori_loop(0, nt, step, init)
    eb_ref[...] = sb


def _carry(hb0, pb0):
    nt = hb0.shape[0] // NV
    return pl.pallas_call(
        functools.partial(_carry_body, nt=nt),
        out_shape=[jax.ShapeDtypeStruct((nt * NV, D_RNN), F32)] * 2
        + [jax.ShapeDtypeStruct((NV, D_RNN), F32)],
        compiler_params=pltpu.CompilerParams(vmem_limit_bytes=VMEM_LIMIT),
        name="carry",
    )(hb0, pb0)


def _mix_out_body(*refs, rows, chunks):
    (gp_ref, g_ref, gn_ref, x_ref, s_ref, pb_ref, gr_ref, cb_ref) = refs[:8]
    refs = refs[8:]
    if chunks > 1:
        qf_ref, qb_ref, ef_ref, eb_ref = refs[:4]
        refs = refs[4:]
    (cw_ref, cb_bias_ref, clg_ref, clb_ref, wo_ref, o_ref, gbuf, cbuf) = refs

    halo = CONV_HALO_ROWS
    i = pl.program_id(0)
    gp = gp_ref[...]
    gn = gn_ref[...]
    gp_edge, gn_edge = _edge_halos(gp, gn, chunks)
    gp = jnp.where(i == 0, gp_edge, gp)
    gn = jnp.where(i == pl.num_programs(0) - 1, gn_edge, gn)
    for col in range(D_CONV // LANES):
        lanes = slice(col * LANES, (col + 1) * LANES)
        gbuf[col, 0:halo, :] = gp[:, lanes]
        gbuf[col, halo:halo + rows, :] = g_ref[:, lanes]
        gbuf[col, halo + rows:, :] = gn[:, lanes]

    half = CONV_WIDTH // 2

    conv_shape = (CONV_ROW_CHUNK // NV, NV, LANES)
    for col in range(D_CONV // LANES):
        lanes = slice(col * LANES, (col + 1) * LANES)
        taps = [cw_ref[k * NV:(k + 1) * NV, lanes][None] for k in range(CONV_WIDTH)]
        bias = jnp.broadcast_to(cb_bias_ref[:, lanes][None], conv_shape)

        def conv_chunk(j, _, col=col, lanes=lanes, taps=taps, bias=bias):
            base = pl.multiple_of(j * CONV_ROW_CHUNK, CONV_ROW_CHUNK)
            acc = [bias, None]
            for k in range(CONV_WIDTH):
                start = base + (halo + (k - half) * NV)
                term = gbuf[col, pl.ds(start, CONV_ROW_CHUNK), :].reshape(conv_shape) * taps[k]
                acc[k % 2] = term if acc[k % 2] is None else acc[k % 2] + term
            total = acc[0] + acc[1]
            cbuf[pl.ds(base, CONV_ROW_CHUNK), lanes] = total.reshape(CONV_ROW_CHUNK, LANES)
            return 0

        lax.fori_loop(0, rows // CONV_ROW_CHUNK, conv_chunk, 0)
    cn = _ln(cbuf[...], clg_ref[...], clb_ref[...])
    c = _times_sigmoid(cn, cn)

    nsteps = rows // NV
    shape3 = (nsteps, NV, D_RNN)
    cb = cb_ref[...]
    rec = s_ref[...].reshape(shape3)
    if chunks > 1:
        ef, eb = _edge_halos(ef_ref[...], eb_ref[...], chunks)
        cb = cb + qb_ref[...] * eb
        rec = rec + qf_ref[...].reshape(shape3) * ef[None]
    rec = rec + pb_ref[...].reshape(shape3) * cb[None]
    rec = rec.reshape(rows, D_RNN) * gr_ref[...]

    mix = (jnp.dot(c.astype(BF16), wo_ref[0:D_CONV, :], preferred_element_type=F32)
           + jnp.dot(rec.astype(BF16), wo_ref[D_CONV:, :], preferred_element_type=F32))
    o_ref[...] = DEEPNORM_ALPHA * x_ref[...] + mix


def _mix_out(x, g, gr, s, pb, cb, extra, p, chunks):
    rows_total = x.shape[0]
    rows = TILE_ROWS
    halo = CONV_HALO_ROWS
    blk = (NV, D_RNN)
    wide = pl.BlockSpec((rows, D_MODEL), lambda i: (i, 0))
    narrow = pl.BlockSpec((rows, D_RNN), lambda i: (i, 0))
    per_tile = pl.BlockSpec(blk, lambda i: (i, 0))
    small = [p["conv_w"], p["conv_b"], p["conv_ln_g"], p["conv_ln_b"], p["w_out"]]
    in_specs = _halo_specs(rows, halo, D_CONV, rows_total) + [wide] + [narrow] * 3 + [per_tile]
    args = [g, g, g, x, s, pb, gr, cb]
    if chunks > 1:
        qf, qb, ef, eb = extra
        in_specs += [narrow, per_tile, _const_spec(blk), _const_spec(blk)]
        args += [qf, qb, ef, eb]
    return pl.pallas_call(
        functools.partial(_mix_out_body, rows=rows, chunks=chunks),
        grid=(rows_total // rows,),
        in_specs=in_specs + [_const_spec(a.shape) for a in small],
        out_specs=wide,
        out_shape=jax.ShapeDtypeStruct((rows_total, D_MODEL), F32),
        scratch_shapes=[
            pltpu.VMEM((D_CONV // LANES, rows + 2 * halo, LANES), F32),
            pltpu.VMEM((rows, D_CONV), F32),
        ],
        compiler_params=_params(),
        name="mix_out",
    )(*args, *small)


def _ffn_body(xp_ref, x_ref, xn_ref, l1g_ref, l1b_ref, wu_ref, dw_ref, db_ref, wd_ref,
              lg_ref, lb_ref, o_ref, xbuf, pbuf, *, rows, chunks, last):
    norm1 = lambda ref: _ln(ref[...], l1g_ref[...], l1b_ref[...])
    _fill_with_halo(xbuf, norm1(xp_ref), norm1(x_ref), norm1(xn_ref), chunks)
    fh = FFN_HALO_ROWS
    xe = xbuf[...].astype(BF16)

    def conv3(u, idx):
        w = dw_ref[idx]
        return (u[0:rows] * w[0:1, :] + u[NV:NV + rows] * w[1:2, :]
                + u[2 * NV:2 * NV + rows] * w[2:3, :] + db_ref[idx])

    for j in range(N_FFN_CHUNKS):
        cols = slice(j * FFN_CHUNK, (j + 1) * FFN_CHUNK)
        gcols = slice(D_FF + j * FFN_CHUNK, D_FF + (j + 1) * FFN_CHUNK)
        uv = jnp.dot(xe, wu_ref[:, cols], preferred_element_type=F32)
        ug = jnp.dot(xe, wu_ref[:, gcols], preferred_element_type=F32)
        val = conv3(uv, j)
        gate = conv3(ug, N_FFN_CHUNKS + j)
        pbuf[:, j * FFN_CHUNK:(j + 1) * FFN_CHUNK] = (_gelu(gate) * val).astype(BF16)
    for r0 in range(0, rows, FFN_OUT_ROWS):
        f = jnp.dot(pbuf[r0:r0 + FFN_OUT_ROWS, :], wd_ref[...], preferred_element_type=F32)
        x1 = xbuf[fh + r0:fh + r0 + FFN_OUT_ROWS, :]
        y = _ln(DEEPNORM_ALPHA * x1 + f, lg_ref[...], lb_ref[...])
        if last:
            t0 = r0 // NV
            o_ref[:, t0:t0 + FFN_OUT_ROWS // NV, :] = _to_natural(y)
        else:
            o_ref[r0:r0 + FFN_OUT_ROWS, :] = y


def _ffn(x, p, chunks, last):
    rows_total = x.shape[0]
    rows = FFN_TILE_ROWS
    small = [p["ln1_g"], p["ln1_b"], p["w_up"], p["ffn_dw_w"], p["ffn_dw_b"], p["w_down"],
             p["ln2_g"], p["ln2_b"]]
    if last:
        out_spec = pl.BlockSpec((NV, rows // NV, D_MODEL), lambda i: (0, i, 0))
        out_shape = jax.ShapeDtypeStruct((NV, rows_total // NV, D_MODEL), F32)
    else:
        out_spec = pl.BlockSpec((rows, D_MODEL), lambda i: (i, 0))
        out_shape = jax.ShapeDtypeStruct((rows_total, D_MODEL), F32)
    return pl.pallas_call(
        functools.partial(_ffn_body, rows=rows, chunks=chunks, last=last),
        grid=(rows_total // rows,),
        in_specs=_halo_specs(rows, FFN_HALO_ROWS, D_MODEL, rows_total)
        + [_const_spec(a.shape) for a in small],
        out_specs=out_spec,
        out_shape=out_shape,
        scratch_shapes=[
            pltpu.VMEM((rows + 2 * FFN_HALO_ROWS, D_MODEL), F32),
            pltpu.VMEM((rows, D_FF), BF16),
        ],
        compiler_params=_params(),
        name="ffn",
    )(x, x, x, *small)


def _block_diag_gate(w_a, w_x):
    heads_per_half = RNN_HEADS // 2
    half_w = D_RNN // 2

    def bd(w):
        eye = jnp.eye(heads_per_half, dtype=w.dtype)
        return jnp.einsum("hij,hg->higj", w, eye).reshape(half_w, half_w)

    out = []
    for d in range(2):
        halves = []
        for hf in range(2):
            sl = slice(hf * heads_per_half, (hf + 1) * heads_per_half)
            halves.append(jnp.concatenate([bd(w_a[d, sl]), bd(w_x[d, sl])], axis=-1))
        out.append(jnp.stack(halves))
    return jnp.stack(out).astype(BF16)


def _layer_params(l, w_in, conv_dw_w, conv_dw_b, conv_ln_g, conv_ln_b, rnn_conv_w, rnn_conv_b,
                  rg_w_a, rg_b_a, rg_w_x, rg_b_x, rg_lambda, w_out, ln1_g, ln1_b, w_up,
                  ffn_dw_w, ffn_dw_b, w_down, ln2_g, ln2_b):
    nc = N_FFN_CHUNKS
    row = lambda a: a.reshape(1, -1)
    return {
        "w_in": w_in[l].astype(BF16),
        "conv_w": jnp.repeat(conv_dw_w[l], NV, axis=0),
        "conv_b": jnp.broadcast_to(row(conv_dw_b[l]), (NV, D_CONV)),
        "conv_ln_g": row(conv_ln_g[l]), "conv_ln_b": row(conv_ln_b[l]),
        "rnn_conv_w": rnn_conv_w[l].reshape(2 * RNN_CONV_WIDTH, D_RNN),
        "rnn_conv_b": rnn_conv_b[l],
        "w_gate": _block_diag_gate(rg_w_a[l], rg_w_x[l]),
        "b_a": rg_b_a[l], "b_x": rg_b_x[l], "lam": rg_lambda[l],
        "w_out": w_out[l].astype(BF16), "ln1_g": row(ln1_g[l]), "ln1_b": row(ln1_b[l]),
        "w_up": w_up[l].astype(BF16),
        "ffn_dw_w": ffn_dw_w[l].reshape(3, 2 * nc, FFN_CHUNK).transpose(1, 0, 2),
        "ffn_dw_b": ffn_dw_b[l].reshape(2 * nc, 1, FFN_CHUNK),
        "w_down": w_down[l].astype(BF16),
        "ln2_g": row(ln2_g[l]), "ln2_b": row(ln2_b[l]),
    }


def _trunk(x, ln_in_g, ln_in_b, layers):
    bsz, seq, _ = x.shape
    chunks = NV // bsz
    sv = seq // chunks
    xt = x.reshape(NV, sv, D_MODEL)
    ln_g = ln_in_g.reshape(1, -1)
    ln_b = ln_in_b.reshape(1, -1)
    for l, p in enumerate(layers):
        outs = _proj_rnn(xt, ln_g, ln_b, p, chunks, first=(l == 0))
        if l == 0:
            xt, outs = outs[0], outs[1:]
        g, gr, s, pb, hb0, pb0 = outs[:6]
        cb, qb, eb = _carry(hb0, pb0)
        extra = (outs[6], qb, outs[7], eb) if chunks > 1 else None
        x1 = _mix_out(xt, g, gr, s, pb, cb, extra, p, chunks)
        xt = _ffn(x1, p, chunks, last=(l == len(layers) - 1))
    return xt.reshape(bsz, seq, D_MODEL)


def kernel(x_prompt, x_sample, ln_in_g, ln_in_b, w_in, conv_dw_w, conv_dw_b, conv_ln_g,
           conv_ln_b, rnn_conv_w, rnn_conv_b, rg_w_a, rg_b_a, rg_w_x, rg_b_x, rg_lambda,
           w_out, ln1_g, ln1_b, w_up, ffn_dw_w, ffn_dw_b, w_down, ln2_g, ln2_b):
    layers = [
        _layer_params(l, w_in, conv_dw_w, conv_dw_b, conv_ln_g, conv_ln_b, rnn_conv_w,
                      rnn_conv_b, rg_w_a, rg_b_a, rg_w_x, rg_b_x, rg_lambda, w_out, ln1_g,
                      ln1_b, w_up, ffn_dw_w, ffn_dw_b, w_down, ln2_g, ln2_b)
        for l in range(DEPTH)
    ]
    y_prompt = _trunk(x_prompt, ln_in_g, ln_in_b, layers)
    y_sample = _trunk(x_sample, ln_in_g, ln_in_b, layers)
    return (y_prompt, y_sample)
```
